```python
import math
import jax
import jax.numpy as jnp
from jax import lax
import numpy as np

D_MODEL = 1024
BATCH = 8
SEQ = 4096
DEPTH = 4

N_MIXERS = 3
PLE_DIM = 256
NORM_EPS = 1e-6
D_FF = ((8 * D_MODEL + 3 * 256 - 1) // (3 * 256)) * 256

SSD_D_INNER = 2 * D_MODEL
SSD_HEADDIM = 64
SSD_HEADS = SSD_D_INNER // SSD_HEADDIM
SSD_GROUPS = 8
SSD_STATE = 128
SSD_CONV = 5
SSD_CHUNK = 128
SSD_CONV_DIM = SSD_D_INNER + 2 * SSD_GROUPS * SSD_STATE
SSD_IN_DIM = SSD_D_INNER + SSD_CONV_DIM + 2 * SSD_HEADS

HY_WIDTH = D_MODEL
HY_SHORT = 3
HY_EMB_DIM = 33
HY_BANDS = (HY_EMB_DIM - 1) // 2
HY_FILTER_HIDDEN = 64
HY_DECAY_TARGET = 1e-2
HY_FAST_DECAY_PCT = 0.3
HY_SLOW_DECAY_PCT = 1.5

GLA_HEADS = 4
GLA_KEY_DIM = D_MODEL // 2
GLA_VALUE_DIM = D_MODEL
GLA_HEAD_K = GLA_KEY_DIM // GLA_HEADS
GLA_HEAD_V = GLA_VALUE_DIM // GLA_HEADS
GLA_GATE_RANK = 16
GLA_GATE_NORMALIZER = 16.0
GLA_CHUNK = 64
GLA_IN_DIM = 2 * GLA_KEY_DIM + 2 * GLA_VALUE_DIM + 2 * GLA_GATE_RANK

kernel_name = 'bidir_hybrid_ssd_hyena_gla_block'


def rms_norm(x, w):
    xf = x.astype(jnp.float32)
    y = xf * lax.rsqrt(jnp.mean(xf * xf, axis=-1, keepdims=True) + NORM_EPS)
    return (y * w.astype(jnp.float32)).astype(x.dtype)


def centred_depthwise_conv(x, w, b):
    k = w.shape[0]
    y = lax.conv_general_dilated(x, w[:, None, :].astype(x.dtype), window_strides=(1,),
                                 padding=[(k // 2, k // 2)],
                                 dimension_numbers=('NWC', 'WIO', 'NWC'),
                                 feature_group_count=x.shape[-1])
    return y + b


def flip_seq(t):
    return jnp.flip(t, axis=1)


def to_chunks(t, size):
    b, l = t.shape[0], t.shape[1]
    return jnp.moveaxis(t.reshape((b, l // size, size) + t.shape[2:]), 1, 0)


def from_chunks(t):
    t = jnp.moveaxis(t, 0, 1)
    return t.reshape((t.shape[0], t.shape[1] * t.shape[2]) + t.shape[3:])


def ssd_chunk_scan(x, dt, a, bm, cm):
    bsz, l, h, p = x.shape
    g, n = bm.shape[2], bm.shape[3]
    k = h // g
    a_gk = a.reshape(g, k)
    mask = jnp.tril(jnp.ones((SSD_CHUNK, SSD_CHUNK), dtype=bool))[None, :, :, None, None]

    def step(state, inp):
        xc, dtc, bc, cc = inp
        acum = jnp.cumsum(dtc * a_gk, axis=1)
        decay = jnp.exp(jnp.where(mask, acum[:, :, None] - acum[:, None], -jnp.inf))
        cb = jnp.einsum('blgn,bsgn->blsg', cc, bc)
        scores = cb[..., None] * decay * dtc[:, None]
        y = jnp.einsum('blsgk,bsgkp->blgkp', scores, xc)
        y = y + jnp.einsum('blgn,bgkpn->blgkp', cc, state) * jnp.exp(acum)[..., None]
        last = acum[:, -1]
        w_s = jnp.exp(last[:, None] - acum) * dtc
        state = state * jnp.exp(last)[..., None, None] + jnp.einsum('bsgn,bsgk,bsgkp->bgkpn', bc, w_s, xc)
        return state, y

    xs = to_chunks(x.reshape(bsz, l, g, k, p), SSD_CHUNK)
    dts = to_chunks(dt.reshape(bsz, l, g, k), SSD_CHUNK)
    state0 = jnp.zeros((bsz, g, k, p, n), jnp.float32)
    _, ys = lax.scan(step, state0, (xs, dts, to_chunks(bm, SSD_CHUNK), to_chunks(cm, SSD_CHUNK)))
    return from_chunks(ys).reshape(bsz, l, h, p)


def mamba2_mixer(h, in_w, conv_w, conv_b, dt_bias, a_log, d_skip, norm_w, out_w):
    bsz, l, _ = h.shape
    zxbcdt = h @ in_w
    z, xbc, dt = jnp.split(zxbcdt, [SSD_D_INNER, SSD_D_INNER + SSD_CONV_DIM], axis=-1)
    xbc = jax.nn.silu(centred_depthwise_conv(xbc, conv_w, conv_b)).astype(jnp.float32)
    xs, bm, cm = jnp.split(xbc, [SSD_D_INNER, SSD_D_INNER + SSD_GROUPS * SSD_STATE], axis=-1)
    xs = xs.reshape(bsz, l, SSD_HEADS, SSD_HEADDIM)
    bm = bm.reshape(bsz, l, SSD_GROUPS, SSD_STATE)
    cm = cm.reshape(bsz, l, SSD_GROUPS, SSD_STATE)
    dt = jax.nn.softplus(dt.astype(jnp.float32).reshape(bsz, l, 2, SSD_HEADS) + dt_bias.astype(jnp.float32))
    a = -jnp.exp(a_log.astype(jnp.float32))
    y_f = ssd_chunk_scan(xs, dt[:, :, 0], a[0], bm, cm)
    y_b = flip_seq(ssd_chunk_scan(flip_seq(xs), flip_seq(dt[:, :, 1]), a[1], flip_seq(bm), flip_seq(cm)))
    y = y_f + y_b + xs * d_skip.astype(jnp.float32)[:, None]
    y = y.reshape(bsz, l, SSD_D_INNER) * jax.nn.silu(z.astype(jnp.float32))
    yg = y.reshape(bsz, l, SSD_GROUPS, SSD_D_INNER // SSD_GROUPS)
    yg = yg * lax.rsqrt(jnp.mean(yg * yg, axis=-1, keepdims=True) + NORM_EPS)
    y = yg.reshape(bsz, l, SSD_D_INNER) * norm_w.astype(jnp.float32)
    return y.astype(h.dtype) @ out_w


def hyena_filters(l, f_w1, f_b1, f_w2, f_b2, f_w3, sin_freq):
    f32 = jnp.float32
    t = jnp.linspace(0.0, 1.0, l, dtype=f32)[:, None]
    w = 2.0 * math.pi * jnp.arange(l, dtype=f32)[:, None] / l
    bands = jnp.linspace(1e-4, HY_BANDS - 1, HY_BANDS, dtype=f32)[None]
    z = jnp.concatenate([t, jnp.cos(bands * w), -jnp.sin(bands * w)], axis=-1)
    freq = sin_freq.astype(f32)
    hid = jnp.sin(freq * (z @ f_w1.astype(f32) + f_b1.astype(f32)))
    hid = jnp.sin(freq * (hid @ f_w2.astype(f32) + f_b2.astype(f32)))
    filt = (hid @ f_w3.astype(f32)).reshape(l, 2, HY_WIDTH)
    min_decay = math.log(HY_DECAY_TARGET) / HY_SLOW_DECAY_PCT
    max_decay = math.log(HY_DECAY_TARGET) / HY_FAST_DECAY_PCT
    deltas = jnp.abs(jnp.linspace(min_decay, max_decay, HY_WIDTH, dtype=f32))
    filt = filt * jnp.exp(-t * deltas)[:, None, :]
    return filt[:, 0], filt[:, 1]


def two_sided_fftconv(u, h_fwd, h_bwd):
    l, c = h_fwd.shape
    filt2 = jnp.concatenate([h_fwd, jnp.zeros((1, c), h_fwd.dtype), jnp.flip(h_bwd[1:], axis=0)], axis=0)
    filt_f = jnp.fft.rfft(filt2, n=2 * l, axis=0)
    u_f = jnp.fft.rfft(u, n=2 * l, axis=1)
    return jnp.fft.irfft(u_f * filt_f[None], n=2 * l, axis=1)[:, :l]


def hyena_mixer(h, in_w, in_b, conv_w, conv_b, f_w1, f_b1, f_w2, f_b2, f_w3, sin_freq, skip, out_w, out_b):
    l = h.shape[1]
    u = centred_depthwise_conv(h @ in_w + in_b, conv_w, conv_b).astype(jnp.float32)
    x0, x1, v = jnp.split(u, 3, axis=-1)
    h_fwd, h_bwd = hyena_filters(l, f_w1, f_b1, f_w2, f_b2, f_w3, sin_freq)
    v = v * x1
    v = two_sided_fftconv(v, h_fwd, h_bwd) + v * skip.astype(jnp.float32)
    y = v * x0
    return y.astype(h.dtype) @ out_w + out_b


def gla_chunk_scan(q, k, v, g):
    bsz, _, h, dk = q.shape
    dv = v.shape[-1]
    mask = jnp.tril(jnp.ones((GLA_CHUNK, GLA_CHUNK), dtype=bool))[None, :, :, None, None]

    def step(state, inp):
        qc, kc, vc, gc = inp
        bc = jnp.cumsum(gc, axis=1)
        decay = jnp.exp(jnp.where(mask, bc[:, :, None] - bc[:, None], -jnp.inf))
        att = jnp.einsum('blhd,bshd,blshd->blsh', qc, kc, decay)
        o = jnp.einsum('blsh,bshv->blhv', att, vc) + jnp.einsum('blhd,bhdv->blhv', qc * jnp.exp(bc), state)
        last = bc[:, -1]
        state = state * jnp.exp(last)[..., None] + jnp.einsum('bshd,bshv->bhdv', kc * jnp.exp(last[:, None] - bc), vc)
        return state, o

    state0 = jnp.zeros((bsz, h, dk, dv), jnp.float32)
    _, o = lax.scan(step, state0, (to_chunks(q, GLA_CHUNK), to_chunks(k, GLA_CHUNK),
                                   to_chunks(v, GLA_CHUNK), to_chunks(g, GLA_CHUNK)))
    return from_chunks(o)


def gla_mixer(h, in_w, gk_w, gk_b, norm_w, out_w):
    bsz, l, _ = h.shape
    f32 = jnp.float32
    proj = h @ in_w
    q, k, v, g, gl_f, gl_b = jnp.split(proj, [GLA_KEY_DIM, 2 * GLA_KEY_DIM,
                                              2 * GLA_KEY_DIM + GLA_VALUE_DIM,
                                              2 * GLA_KEY_DIM + 2 * GLA_VALUE_DIM,
                                              2 * GLA_KEY_DIM + 2 * GLA_VALUE_DIM + GLA_GATE_RANK], axis=-1)
    q = q.astype(f32).reshape(bsz, l, GLA_HEADS, GLA_HEAD_K) * (GLA_HEAD_K ** -0.5)
    k = k.astype(f32).reshape(bsz, l, GLA_HEADS, GLA_HEAD_K)
    v = v.astype(f32).reshape(bsz, l, GLA_HEADS, GLA_HEAD_V)

    def log_gate(gl, w, b):
        return (jax.nn.log_sigmoid((gl @ w + b).astype(f32)) / GLA_GATE_NORMALIZER).reshape(bsz, l, GLA_HEADS, GLA_HEAD_K)

    g_f = log_gate(gl_f, gk_w[0], gk_b[0])
    g_b = log_gate(gl_b, gk_w[1], gk_b[1])
    o = gla_chunk_scan(q, k, v, g_f) + flip_seq(gla_chunk_scan(flip_seq(q), flip_seq(k), flip_seq(v), flip_seq(g_b)))
    o = rms_norm(o, norm_w).reshape(bsz, l, GLA_VALUE_DIM) * jax.nn.silu(g.astype(f32))
    return o.astype(h.dtype) @ out_w


def swiglu(h, w1, w3, w2):
    return (jax.nn.silu(h @ w1) * (h @ w3)) @ w2


def setup_inputs(seed: int = 0) -> dict:
    key = jax.random.key(seed)
    ks = iter(jax.random.split(key, 64))
    f32 = jnp.float32
    D = D_MODEL
    n_a = len(range(0, DEPTH, N_MIXERS))
    n_b = len(range(1, DEPTH, N_MIXERS))
    n_c = len(range(2, DEPTH, N_MIXERS))

    def nrm(shape, scale):
        return jax.random.normal(next(ks), shape, f32) * scale

    def gain(shape):
        return 1.0 + 0.02 * jax.random.normal(next(ks), shape, f32)

    def dt_bias_init(shape):
        dt = jnp.exp(jax.random.uniform(next(ks), shape, f32, math.log(1e-3), math.log(1e-1)))
        return dt + jnp.log(-jnp.expm1(-dt))

    return {
        'x': nrm((BATCH, SEQ, D), 1.0),
        'p': nrm((DEPTH, BATCH, SEQ, PLE_DIM), 1.0),
        'norm_mix': gain((DEPTH, D)),
        'norm_ffn': gain((DEPTH, D)),
        'norm_ple': gain((DEPTH, D)),
        'ple_gate': nrm((DEPTH, D, D), D ** -0.5),
        'ple_proj': nrm((DEPTH, PLE_DIM, D), PLE_DIM ** -0.5),
        'ffn_w1': nrm((DEPTH, D, D_FF), D ** -0.5),
        'ffn_w3': nrm((DEPTH, D, D_FF), D ** -0.5),
        'ffn_w2': nrm((DEPTH, D_FF, D), D_FF ** -0.5),
        'final_norm': gain((D,)),
        'ssd_in_w': nrm((n_a, D, SSD_IN_DIM), D ** -0.5),
        'ssd_conv_w': nrm((n_a, SSD_CONV, SSD_CONV_DIM), SSD_CONV ** -0.5),
        'ssd_conv_b': nrm((n_a, SSD_CONV_DIM), 0.02),
        'ssd_dt_bias': dt_bias_init((n_a, 2, SSD_HEADS)),
        'ssd_a_log': jnp.log(jax.random.uniform(next(ks), (n_a, 2, SSD_HEADS), f32, 1.0, 16.0)),
        'ssd_d': 1.0 + nrm((n_a, SSD_HEADS), 0.1),
        'ssd_norm': gain((n_a, SSD_D_INNER)),
        'ssd_out_w': nrm((n_a, SSD_D_INNER, D), SSD_D_INNER ** -0.5),
        'hy_in_w': nrm((n_b, D, 3 * HY_WIDTH), D ** -0.5),
        'hy_in_b': nrm((n_b, 3 * HY_WIDTH), 0.02),
        'hy_conv_w': nrm((n_b, HY_SHORT, 3 * HY_WIDTH), HY_SHORT ** -0.5),
        'hy_conv_b': nrm((n_b, 3 * HY_WIDTH), 0.02),
        'hy_f_w1': nrm((n_b, HY_EMB_DIM, HY_FILTER_HIDDEN), HY_EMB_DIM ** -0.5),
        'hy_f_b1': nrm((n_b, HY_FILTER_HIDDEN), 0.02),
        'hy_f_w2': nrm((n_b, HY_FILTER_HIDDEN, HY_FILTER_HIDDEN), HY_FILTER_HIDDEN ** -0.5),
        'hy_f_b2': nrm((n_b, HY_FILTER_HIDDEN), 0.02),
        'hy_f_w3': nrm((n_b, HY_FILTER_HIDDEN, 2 * HY_WIDTH), 0.005),
        'hy_sin_freq': 1.0 + nrm((n_b, HY_FILTER_HIDDEN), 0.1),
        'hy_skip': nrm((n_b, HY_WIDTH), 0.5),
        'hy_out_w': nrm((n_b, HY_WIDTH, D), HY_WIDTH ** -0.5),
        'hy_out_b': nrm((n_b, D), 0.02),
        'gla_in_w': nrm((n_c, D, GLA_IN_DIM), D ** -0.5),
        'gla_gk_w': nrm((n_c, 2, GLA_GATE_RANK, GLA_KEY_DIM), GLA_GATE_RANK ** -0.5),
        'gla_gk_b': nrm((n_c, 2, GLA_KEY_DIM), 0.1),
        'gla_norm': gain((n_c, GLA_HEAD_V)),
        'gla_out_w': nrm((n_c, GLA_VALUE_DIM, D), GLA_VALUE_DIM ** -0.5),
    }


def reference(x, p, norm_mix, norm_ffn, norm_ple, ple_gate, ple_proj, ffn_w1, ffn_w3, ffn_w2, final_norm,
              ssd_in_w, ssd_conv_w, ssd_conv_b, ssd_dt_bias, ssd_a_log, ssd_d, ssd_norm, ssd_out_w,
              hy_in_w, hy_in_b, hy_conv_w, hy_conv_b, hy_f_w1, hy_f_b1, hy_f_w2, hy_f_b2, hy_f_w3,
              hy_sin_freq, hy_skip, hy_out_w, hy_out_b,
              gla_in_w, gla_gk_w, gla_gk_b, gla_norm, gla_out_w):
    h = x
    for i in range(DEPTH):
        kind = i % N_MIXERS
        j = i // N_MIXERS
        hn = rms_norm(h, norm_mix[i])
        if kind == 0:
            mix = mamba2_mixer(hn, ssd_in_w[j], ssd_conv_w[j], ssd_conv_b[j], ssd_dt_bias[j],
                               ssd_a_log[j], ssd_d[j], ssd_norm[j], ssd_out_w[j])
        elif kind == 1:
            mix = hyena_mixer(hn, hy_in_w[j], hy_in_b[j], hy_conv_w[j], hy_conv_b[j], hy_f_w1[j], hy_f_b1[j],
                              hy_f_w2[j], hy_f_b2[j], hy_f_w3[j], hy_sin_freq[j], hy_skip[j],
                              hy_out_w[j], hy_out_b[j])
        else:
            mix = gla_mixer(hn, gla_in_w[j], gla_gk_w[j], gla_gk_b[j], gla_norm[j], gla_out_w[j])
        h = h + mix
        h = h + swiglu(rms_norm(h, norm_ffn[i]), ffn_w1[i], ffn_w3[i], ffn_w2[i])
        gate = jax.nn.sigmoid(rms_norm(h, norm_ple[i]) @ ple_gate[i])
        h = h + gate * (p[i] @ ple_proj[i])
    return rms_norm(h, final_norm)
```

```python
import functools
import math

import numpy as np
import jax
import jax.numpy as jnp
from jax import lax
from jax.experimental import pallas as pl
from jax.experimental.pallas import tpu as pltpu

F32 = jnp.float32
MXU_DTYPE = jnp.bfloat16
EXACT = lax.Precision.HIGHEST

NORM_EPS = 1e-6
N_MIXERS = 3

VMEM_LIMIT_BYTES = 56 * 1024 * 1024
ROW_TILE = 512

SSD_HEADDIM = 64
SSD_GROUPS = 8
SSD_STATE = 128
SSD_CHUNK = 128
GLA_HEADS = 4
GLA_CHUNK = 64
GLA_SUB = 16
GLA_GATE_RANK = 16
GLA_GATE_NORMALIZER = 16.0
HY_BANDS = 16
HY_DECAY_TARGET = 1e-2
HY_FAST_DECAY_PCT = 0.3
HY_SLOW_DECAY_PCT = 1.5
FFT_N1 = 64
LANES = 128


def _cparams(sem):
    return pltpu.CompilerParams(dimension_semantics=sem, vmem_limit_bytes=VMEM_LIMIT_BYTES)


def _dot(a, b):
    return jnp.dot(a.astype(MXU_DTYPE), b.astype(MXU_DTYPE), preferred_element_type=F32)


def _dot_nt(a, b):
    return lax.dot_general(a.astype(MXU_DTYPE), b.astype(MXU_DTYPE), (((1,), (1,)), ((), ())),
                           preferred_element_type=F32)


def _dot_tn(a, b):
    return lax.dot_general(a.astype(MXU_DTYPE), b.astype(MXU_DTYPE), (((0,), (0,)), ((), ())),
                           preferred_element_type=F32)


def _dot_exact(a, b):
    return jnp.dot(a, b, preferred_element_type=F32, precision=EXACT)


def _rms(x, w):
    return x * lax.rsqrt(jnp.mean(x * x, axis=-1, keepdims=True) + NORM_EPS) * w


def _sigmoid(x):
    return 1.0 / (1.0 + jnp.exp(-x))


def _silu(x):
    return x * _sigmoid(x)


def _softplus(x):
    return jnp.maximum(x, 0.0) + jnp.log(1.0 + jnp.exp(-jnp.abs(x)))


def _norm_linear_kernel(x_ref, nw_ref, w_ref, b_ref, o_ref, xn_ref):
    @pl.when(pl.program_id(1) == 0)
    def _():
        xn_ref[...] = _rms(x_ref[...], nw_ref[...]).astype(xn_ref.dtype)

    o_ref[...] = jnp.dot(xn_ref[...], w_ref[...], preferred_element_type=F32) + b_ref[...]


def norm_linear(x, nw, w, b, col_tile):
    t, d = x.shape
    n = w.shape[1]
    tm = min(ROW_TILE, t)
    return pl.pallas_call(
        _norm_linear_kernel,
        grid=(t // tm, n // col_tile),
        in_specs=[pl.BlockSpec((tm, d), lambda i, j: (i, 0)),
                  pl.BlockSpec((1, d), lambda i, j: (0, 0)),
                  pl.BlockSpec((d, col_tile), lambda i, j: (0, j)),
                  pl.BlockSpec((1, col_tile), lambda i, j: (0, j))],
        out_specs=pl.BlockSpec((tm, col_tile), lambda i, j: (i, j)),
        out_shape=jax.ShapeDtypeStruct((t, n), F32),
        scratch_shapes=[pltpu.VMEM((tm, d), MXU_DTYPE)],
        compiler_params=_cparams(("parallel", "arbitrary")),
        name="norm_linear",
    )(x, nw, w, b)


def _ffn_ple_kernel(h_ref, nf_ref, w1_ref, w3_ref, w2_ref, p_ref, npl_ref, g_ref, pp_ref, fn_ref,
                    o_ref, xn_ref, acc_ref, *, final):
    f = pl.program_id(1)

    @pl.when(f == 0)
    def _():
        xn_ref[...] = _rms(h_ref[...], nf_ref[...]).astype(xn_ref.dtype)
        acc_ref[...] = jnp.zeros_like(acc_ref)

    xn = xn_ref[...]
    a = jnp.dot(xn, w1_ref[...], preferred_element_type=F32)
    b = jnp.dot(xn, w3_ref[...], preferred_element_type=F32)
    acc_ref[...] += _dot(_silu(a) * b, w2_ref[...])

    @pl.when(f == pl.num_programs(1) - 1)
    def _():
        h2 = h_ref[...] + acc_ref[...]
        gate = _sigmoid(_dot(_rms(h2, npl_ref[...]), g_ref[...]))
        h3 = h2 + gate * _dot(p_ref[...], pp_ref[...])
        if final:
            h3 = _rms(h3, fn_ref[...])
        o_ref[...] = h3


def ffn_ple(h, nf, w1, w3, w2, p, npl, g, pp, fn, final):
    t, d = h.shape
    dff = w1.shape[1]
    tf = dff // 2
    pdim = p.shape[1]
    tm = min(ROW_TILE, t)
    row = lambda i, f: (i, 0)
    const = lambda i, f: (0, 0)
    return pl.pallas_call(
        functools.partial(_ffn_ple_kernel, final=final),
        grid=(t // tm, dff // tf),
        in_specs=[pl.BlockSpec((tm, d), row),
                  pl.BlockSpec((1, d), const),
                  pl.BlockSpec((d, tf), lambda i, f: (0, f)),
                  pl.BlockSpec((d, tf), lambda i, f: (0, f)),
                  pl.BlockSpec((tf, d), lambda i, f: (f, 0)),
                  pl.BlockSpec((tm, pdim), row),
                  pl.BlockSpec((1, d), const),
                  pl.BlockSpec((d, d), const),
                  pl.BlockSpec((pdim, d), const),
                  pl.BlockSpec((1, d), const)],
        out_specs=pl.BlockSpec((tm, d), row),
        out_shape=jax.ShapeDtypeStruct((t, d), F32),
        scratch_shapes=[pltpu.VMEM((tm, d), MXU_DTYPE), pltpu.VMEM((tm, d), F32)],
        compiler_params=_cparams(("parallel", "arbitrary")),
        name="ffn_ple",
    )(h, nf, w1, w3, w2, p, npl, g, pp, fn)


CONV_PAD = 8
CONV_ROWS = 256


def _conv_taps(pad_ref, w_ref, b_ref, base, rows, taps):
    win = pad_ref[pl.ds(base, rows + 2 * CONV_PAD), :]
    acc = b_ref[...]
    for k in range(taps):
        lo = CONV_PAD + k - taps // 2
        acc = acc + w_ref[k:k + 1, :] * win[lo:lo + rows, :]
    return acc


def _fill_padded(pad_ref, x_ref, seq):
    zeros = jnp.zeros((CONV_PAD, pad_ref.shape[1]), F32)
    pad_ref[0:CONV_PAD, :] = zeros
    pad_ref[CONV_PAD + seq:2 * CONV_PAD + seq, :] = zeros
    pad_ref[CONV_PAD:CONV_PAD + seq, :] = x_ref[...]


def _ssd_conv_kernel(x_ref, w_ref, b_ref, o_ref, pad_ref, *, taps):
    seq = x_ref.shape[0]
    rows = min(CONV_ROWS, seq)
    _fill_padded(pad_ref, x_ref, seq)

    def body(r, carry):
        base = pl.multiple_of(r * rows, rows)
        o_ref[pl.ds(base, rows), :] = _silu(_conv_taps(pad_ref, w_ref, b_ref, base, rows, taps))
        return carry

    lax.fori_loop(0, seq // rows, body, 0)


def ssd_conv(zx, w, b, col_off, tc=256):
    bsz, seq, _ = zx.shape
    taps, c = w.shape
    off = col_off // tc
    return pl.pallas_call(
        functools.partial(_ssd_conv_kernel, taps=taps),
        grid=(bsz, c // tc),
        in_specs=[pl.BlockSpec((None, seq, tc), lambda i, j: (i, 0, j + off)),
                  pl.BlockSpec((taps, tc), lambda i, j: (0, j)),
                  pl.BlockSpec((1, tc), lambda i, j: (0, j))],
        out_specs=pl.BlockSpec((None, seq, tc), lambda i, j: (i, 0, j)),
        out_shape=jax.ShapeDtypeStruct((bsz, seq, c), F32),
        scratch_shapes=[pltpu.VMEM((seq + 2 * CONV_PAD, tc), F32)],
        compiler_params=_cparams(("parallel", "parallel")),
        name="ssd_conv",
    )(zx, w, b)


def _hy_conv_kernel(x0_ref, x1_ref, v_ref, w0_ref, w1_ref, wv_ref, b0_ref, b1_ref, bv_ref,
                    x0c_ref, vx_ref, p0_ref, p1_ref, pv_ref, *, taps):
    seq = x0_ref.shape[0]
    rows = min(CONV_ROWS, seq)
    _fill_padded(p0_ref, x0_ref, seq)
    _fill_padded(p1_ref, x1_ref, seq)
    _fill_padded(pv_ref, v_ref, seq)

    def body(r, carry):
        base = pl.multiple_of(r * rows, rows)
        x0c_ref[pl.ds(base, rows), :] = _conv_taps(p0_ref, w0_ref, b0_ref, base, rows, taps)
        vx_ref[pl.ds(base, rows), :] = (_conv_taps(pv_ref, wv_ref, bv_ref, base, rows, taps)
                                        * _conv_taps(p1_ref, w1_ref, b1_ref, base, rows, taps))
        return carry

    lax.fori_loop(0, seq // rows, body, 0)


def hy_conv(u, w, b, tc=256):
    bsz, seq, c3 = u.shape
    c = c3 // 3
    taps = w.shape[0]
    nb = c // tc
    xspec = lambda part: pl.BlockSpec((None, seq, tc), lambda i, j: (i, 0, j + part * nb))
    wspec = lambda part: pl.BlockSpec((taps, tc), lambda i, j: (0, j + part * nb))
    bspec = lambda part: pl.BlockSpec((1, tc), lambda i, j: (0, j + part * nb))
    ospec = pl.BlockSpec((None, seq, tc), lambda i, j: (i, 0, j))
    pad = pltpu.VMEM((seq + 2 * CONV_PAD, tc), F32)
    return pl.pallas_call(
        functools.partial(_hy_conv_kernel, taps=taps),
        grid=(bsz, nb),
        in_specs=[xspec(0), xspec(1), xspec(2), wspec(0), wspec(1), wspec(2), bspec(0), bspec(1), bspec(2)],
        out_specs=[ospec, ospec],
        out_shape=[jax.ShapeDtypeStruct((bsz, seq, c), F32)] * 2,
        scratch_shapes=[pad, pad, pad],
        compiler_params=_cparams(("parallel", "parallel")),
        name="hy_conv",
    )(u, u, u, w, w, w, b, b, b)


def _tri(n, reverse):
    row = lax.broadcasted_iota(jnp.int32, (n, n), 0)
    col = lax.broadcasted_iota(jnp.int32, (n, n), 1)
    return (row <= col) if reverse else (row >= col)


def _ssd_direction(xs_ref, b_ref, c_ref, dt_ref, dtb_ref, alog_ref, st_ref, y_ref, *, lane0, reverse):
    t = SSD_CHUNK
    heads = xs_ref.shape[1] // SSD_HEADDIM
    per_group = heads // SSD_GROUPS
    gw = per_group * SSD_HEADDIM
    tri = _tri(t, reverse)
    tri_f = tri.astype(F32)

    dt_all = _softplus(dt_ref[...] + dtb_ref[...])
    da_all = dt_all * (-jnp.exp(alog_ref[...]))
    dt = dt_all[:, lane0:lane0 + heads]
    da = da_all[:, lane0:lane0 + heads]
    dt_t = dt_all.T[lane0:lane0 + heads, :]
    da_t = da_all.T[lane0:lane0 + heads, :]
    acum = _dot_exact(tri_f, da)
    acum_t = _dot_exact(da_t, tri_f.T)
    tot = jnp.sum(da, axis=0, keepdims=True)

    hrow = lax.broadcasted_iota(jnp.int32, (heads, heads * SSD_HEADDIM), 0)
    hcol = lax.broadcasted_iota(jnp.int32, (heads, heads * SSD_HEADDIM), 1) // SSD_HEADDIM
    expand = (hrow == hcol).astype(F32)
    eacum_x = _dot_exact(jnp.exp(acum), expand)
    w_x = _dot_exact(jnp.exp(tot - acum) * dt, expand)
    etot_x = _dot_exact(jnp.broadcast_to(jnp.exp(tot), (8, heads)), expand)[0:1, :]

    xs = xs_ref[...]
    xw = (xs * w_x).astype(MXU_DTYPE)
    lane = lax.broadcasted_iota(jnp.int32, (t, 2 * SSD_HEADDIM), 1)
    for g in range(SSD_GROUPS):
        bg_t = b_ref[:, g * SSD_STATE:(g + 1) * SSD_STATE].T.astype(MXU_DTYPE)
        cg = c_ref[:, g * SSD_STATE:(g + 1) * SSD_STATE].astype(MXU_DTYPE)
        cb = jnp.dot(cg, bg_t, preferred_element_type=F32)
        st = st_ref[g]
        gl = slice(g * gw, (g + 1) * gw)
        y_inter = _dot(cg, st) * eacum_x[:, gl]
        st_ref[g] = st * etot_x[:, gl] + jnp.dot(bg_t, xw[:, gl], preferred_element_type=F32)
        parts = []
        for pr in range(per_group // 2):
            scores = []
            for k in range(2):
                h = g * per_group + 2 * pr + k
                diff = acum[:, h:h + 1] - acum_t[h:h + 1, :]
                decay = jnp.exp(jnp.where(tri, diff, -jnp.inf))
                scores.append(cb * decay * dt_t[h:h + 1, :])
            x_pair = xs[:, g * gw + pr * 2 * SSD_HEADDIM:g * gw + (pr + 1) * 2 * SSD_HEADDIM]
            rhs = jnp.concatenate([jnp.where(lane < SSD_HEADDIM, x_pair, 0.0),
                                   jnp.where(lane >= SSD_HEADDIM, x_pair, 0.0)], axis=0)
            parts.append(_dot(jnp.concatenate(scores, axis=1), rhs))
        y_ref[:, gl] = y_inter + jnp.concatenate(parts, axis=1)


def _ssd_scan_kernel(xf_ref, bf_ref, cf_ref, dtf_ref, xb_ref, bb_ref, cb_ref, dtb_ref,
                     bias_ref, alog_ref, yf_ref, yb_ref, stf_ref, stb_ref):
    @pl.when(pl.program_id(1) == 0)
    def _():
        stf_ref[...] = jnp.zeros_like(stf_ref)
        stb_ref[...] = jnp.zeros_like(stb_ref)

    heads = xf_ref.shape[1] // SSD_HEADDIM
    _ssd_direction(xf_ref, bf_ref, cf_ref, dtf_ref, bias_ref, alog_ref, stf_ref, yf_ref, lane0=0, reverse=False)
    _ssd_direction(xb_ref, bb_ref, cb_ref, dtb_ref, bias_ref, alog_ref, stb_ref, yb_ref, lane0=heads, reverse=True)


def ssd_scan(xbc, dt_raw, dt_bias, a_log):
    bsz, seq, width = xbc.shape
    gn = SSD_GROUPS * SSD_STATE
    d_inner = width - 2 * gn
    t = SSD_CHUNK
    nc = seq // t
    xblk = d_inner // gn
    fwd = lambda b, c: (b, c)
    bwd = lambda b, c: (b, nc - 1 - c)

    def specs(pos):
        return [pl.BlockSpec((None, t, d_inner), lambda b, c: (*pos(b, c), 0)),
                pl.BlockSpec((None, t, gn), lambda b, c: (*pos(b, c), xblk)),
                pl.BlockSpec((None, t, gn), lambda b, c: (*pos(b, c), xblk + 1)),
                pl.BlockSpec((None, t, LANES), lambda b, c: (*pos(b, c), 0))]

    const = pl.BlockSpec((1, LANES), lambda b, c: (0, 0))
    state = pltpu.VMEM((SSD_GROUPS, SSD_STATE, d_inner // SSD_GROUPS), F32)
    return pl.pallas_call(
        _ssd_scan_kernel,
        grid=(bsz, nc),
        in_specs=specs(fwd) + specs(bwd) + [const, const],
        out_specs=[pl.BlockSpec((None, t, d_inner), lambda b, c: (*fwd(b, c), 0)),
                   pl.BlockSpec((None, t, d_inner), lambda b, c: (*bwd(b, c), 0))],
        out_shape=[jax.ShapeDtypeStruct((bsz, seq, d_inner), F32)] * 2,
        scratch_shapes=[state, state],
        compiler_params=_cparams(("parallel", "arbitrary")),
        name="ssd_scan",
    )(xbc, xbc, xbc, dt_raw, xbc, xbc, xbc, dt_raw, dt_bias, a_log)


def _ssd_out_kernel(yf_ref, yb_ref, xs_ref, z_ref, dsk_ref, nw_ref, w_ref, res_ref, o_ref):
    y = (yf_ref[...] + yb_ref[...] + xs_ref[...] * dsk_ref[...]) * _silu(z_ref[...])
    gw = y.shape[1] // SSD_GROUPS
    y = jnp.concatenate([_rms(y[:, g * gw:(g + 1) * gw], nw_ref[:, g * gw:(g + 1) * gw])
                         for g in range(SSD_GROUPS)], axis=1)
    o_ref[...] = res_ref[...] + _dot(y, w_ref[...])


def ssd_out(y_f, y_b, xbc, zx, d_skip, norm_w, out_w, res):
    t, d_inner = y_f.shape
    d = out_w.shape[1]
    tm = min(ROW_TILE, t)
    row = lambda i: (i, 0)
    const = lambda i: (0, 0)
    return pl.pallas_call(
        _ssd_out_kernel,
        grid=(t // tm,),
        in_specs=[pl.BlockSpec((tm, d_inner), row), pl.BlockSpec((tm, d_inner), row),
                  pl.BlockSpec((tm, d_inner), row), pl.BlockSpec((tm, d_inner), row),
                  pl.BlockSpec((1, d_inner), const), pl.BlockSpec((1, d_inner), const),
                  pl.BlockSpec((d_inner, d), const), pl.BlockSpec((tm, d), row)],
        out_specs=pl.BlockSpec((tm, d), row),
        out_shape=jax.ShapeDtypeStruct((t, d), F32),
        compiler_params=_cparams(("parallel",)),
        name="ssd_out",
    )(y_f, y_b, xbc, zx, d_skip, norm_w, out_w, res)


def _gla_direction(q_ref, k_ref, v_ref, gl_ref, wgk_ref, bgk_ref, st_ref, o_ref, *, d, reverse):
    ch = GLA_CHUNK
    kd = q_ref.shape[1]
    dk = kd // GLA_HEADS
    dv = v_ref.shape[1] // GLA_HEADS
    tri = _tri(ch, reverse)
    x = _dot(gl_ref[...], wgk_ref[:, d * kd:(d + 1) * kd]) + bgk_ref[:, d * kd:(d + 1) * kd]
    g = -_softplus(-x) * (1.0 / GLA_GATE_NORMALIZER)
    bc = _dot_exact(tri.astype(F32), g)
    tot = jnp.sum(g, axis=0, keepdims=True)
    q = q_ref[...] * (dk ** -0.5)
    k = k_ref[...]
    v = v_ref[...].astype(MXU_DTYPE)
    q_in = q * jnp.exp(bc)
    k_st = k * jnp.exp(tot - bc)
    e_tot = jnp.exp(tot)
    nsub = ch // GLA_SUB
    for h in range(GLA_HEADS):
        ks = slice(h * dk, (h + 1) * dk)
        vs = slice(h * dv, (h + 1) * dv)
        st = st_ref[h]
        o_inter = _dot_nt(q_in[:, ks], st)
        blocks = [None] * nsub
        for i in range(nsub):
            l0, l1 = i * GLA_SUB, (i + 1) * GLA_SUB
            if reverse:
                s0, s1 = l0, ch
                ref = bc[l1:l1 + 1, ks] if l1 < ch else 0.0
            else:
                s0, s1 = 0, l1
                ref = bc[l0 - 1:l0, ks] if l0 > 0 else 0.0
            qt = q[l0:l1, ks] * jnp.exp(bc[l0:l1, ks] - ref)
            kt = k[s0:s1, ks] * jnp.exp(ref - bc[s0:s1, ks])
            att = _dot_nt(qt, kt)
            lpos = l0 + lax.broadcasted_iota(jnp.int32, att.shape, 0)
            spos = s0 + lax.broadcasted_iota(jnp.int32, att.shape, 1)
            att = jnp.where((lpos <= spos) if reverse else (lpos >= spos), att, 0.0)
            blocks[i] = _dot(att, v[s0:s1, vs])
        o_ref[:, vs] = o_inter + jnp.concatenate(blocks, axis=0)
        st_ref[h] = st * e_tot[:, ks] + _dot_tn(v[:, vs], k_st[:, ks])


def _gla_scan_kernel(qf_ref, kf_ref, vf_ref, glf_ref, qb_ref, kb_ref, vb_ref, glb_ref, wgk_ref, bgk_ref,
                     of_ref, ob_ref, stf_ref, stb_ref):
    @pl.when(pl.program_id(1) == 0)
    def _():
        stf_ref[...] = jnp.zeros_like(stf_ref)
        stb_ref[...] = jnp.zeros_like(stb_ref)

    _gla_direction(qf_ref, kf_ref, vf_ref, glf_ref, wgk_ref, bgk_ref, stf_ref, of_ref, d=0, reverse=False)
    _gla_direction(qb_ref, kb_ref, vb_ref, glb_ref, wgk_ref, bgk_ref, stb_ref, ob_ref, d=1, reverse=True)


def gla_scan(qkvg, gl, wgk, bgk, key_dim, value_dim):
    bsz, seq, _ = qkvg.shape
    ch = GLA_CHUNK
    nc = seq // ch
    fwd = lambda b, c: (b, c)
    bwd = lambda b, c: (b, nc - 1 - c)

    def specs(pos):
        return [pl.BlockSpec((None, ch, key_dim), lambda b, c: (*pos(b, c), 0)),
                pl.BlockSpec((None, ch, key_dim), lambda b, c: (*pos(b, c), 1)),
                pl.BlockSpec((None, ch, value_dim), lambda b, c: (*pos(b, c), (2 * key_dim) // value_dim)),
                pl.BlockSpec((None, ch, LANES), lambda b, c: (*pos(b, c), 0))]

    state = pltpu.VMEM((GLA_HEADS, value_dim // GLA_HEADS, key_dim // GLA_HEADS), F32)
    return pl.pallas_call(
        _gla_scan_kernel,
        grid=(bsz, nc),
        in_specs=specs(fwd) + specs(bwd) + [pl.BlockSpec((LANES, 2 * key_dim), lambda b, c: (0, 0)),
                                            pl.BlockSpec((1, 2 * key_dim), lambda b, c: (0, 0))],
        out_specs=[pl.BlockSpec((None, ch, value_dim), lambda b, c: (*fwd(b, c), 0)),
                   pl.BlockSpec((None, ch, value_dim), lambda b, c: (*bwd(b, c), 0))],
        out_shape=[jax.ShapeDtypeStruct((bsz, seq, value_dim), F32)] * 2,
        scratch_shapes=[state, state],
        compiler_params=_cparams(("parallel", "arbitrary")),
        name="gla_scan",
    )(qkvg, qkvg, qkvg, gl, qkvg, qkvg, qkvg, gl, wgk, bgk)


def _gla_out_kernel(of_ref, ob_ref, g_ref, nw_ref, w_ref, res_ref, o_ref):
    o = of_ref[...] + ob_ref[...]
    dv = o.shape[1] // GLA_HEADS
    o = jnp.concatenate([_rms(o[:, h * dv:(h + 1) * dv], nw_ref[...]) for h in range(GLA_HEADS)], axis=1)
    o_ref[...] = res_ref[...] + _dot(o * _silu(g_ref[...]), w_ref[...])


def gla_out(o_f, o_b, qkvg, norm_w, out_w, res, g_col):
    t, vd = o_f.shape
    d = out_w.shape[1]
    tm = min(ROW_TILE, t)
    row = lambda i: (i, 0)
    const = lambda i: (0, 0)
    return pl.pallas_call(
        _gla_out_kernel,
        grid=(t // tm,),
        in_specs=[pl.BlockSpec((tm, vd), row), pl.BlockSpec((tm, vd), row),
                  pl.BlockSpec((tm, vd), lambda i: (i, g_col // vd)),
                  pl.BlockSpec((1, vd // GLA_HEADS), const),
                  pl.BlockSpec((vd, d), const), pl.BlockSpec((tm, d), row)],
        out_specs=pl.BlockSpec((tm, d), row),
        out_shape=jax.ShapeDtypeStruct((t, d), F32),
        compiler_params=_cparams(("parallel",)),
        name="gla_out",
    )(o_f, o_b, qkvg, norm_w, out_w, res)


def _hy_filter_kernel(w1_ref, b1_ref, w2_ref, b2_ref, w3_ref, freq_ref, delta_ref, o_ref, *, seq):
    tr = o_ref.shape[0]
    r = pl.program_id(1) * tr + lax.broadcasted_iota(jnp.int32, (tr, 1), 0)
    pos = jnp.where(r < seq, r, 2 * seq - r).astype(F32)
    t = pos / (seq - 1.0)
    w = (2.0 * math.pi) * pos / seq
    band = lax.broadcasted_iota(jnp.int32, (1, HY_BANDS), 1).astype(F32)
    bands = 1e-4 + band * ((HY_BANDS - 1 - 1e-4) / (HY_BANDS - 1))
    zw1 = (t * w1_ref[0:1, :] + _dot_exact(jnp.cos(bands * w), w1_ref[1:1 + HY_BANDS, :])
           + _dot_exact(-jnp.sin(bands * w), w1_ref[1 + HY_BANDS:1 + 2 * HY_BANDS, :]))
    freq = freq_ref[...]
    hid = jnp.sin(freq * (zw1 + b1_ref[...]))
    hid = jnp.sin(freq * (_dot_exact(hid, w2_ref[...]) + b2_ref[...]))
    filt = _dot_exact(hid, w3_ref[...]) * jnp.exp(-t * delta_ref[...])
    o_ref[...] = jnp.where(r == seq, 0.0, filt)


def hy_filter(seq, w1, b1, w2, b2, w3, freq, width, tr=512):
    n = 2 * seq
    tr = min(tr, seq)
    tc = 512
    emb, hidden = w1.shape
    min_decay = math.log(HY_DECAY_TARGET) / HY_SLOW_DECAY_PCT
    max_decay = math.log(HY_DECAY_TARGET) / HY_FAST_DECAY_PCT
    deltas = jnp.abs(jnp.linspace(min_decay, max_decay, width, dtype=F32))[None, :]
    nct = width // tc
    half = seq // tr
    const = lambda j, i: (0, 0)
    return pl.pallas_call(
        functools.partial(_hy_filter_kernel, seq=seq),
        grid=(nct, n // tr),
        in_specs=[pl.BlockSpec((emb, hidden), const), pl.BlockSpec((1, hidden), const),
                  pl.BlockSpec((hidden, hidden), const), pl.BlockSpec((1, hidden), const),
                  pl.BlockSpec((hidden, tc), lambda j, i: (0, j + nct * (i // half))),
                  pl.BlockSpec((1, hidden), const),
                  pl.BlockSpec((1, tc), lambda j, i: (0, j))],
        out_specs=pl.BlockSpec((tr, tc), lambda j, i: (i, j)),
        out_shape=jax.ShapeDtypeStruct((n, width), F32),
        compiler_params=_cparams(("parallel", "parallel")),
        name="hy_filter",
    )(w1, b1, w2, b2, w3, freq, deltas)


def _stack_complex(m):
    return np.concatenate([np.concatenate([m.real, -m.imag], axis=-1),
                           np.concatenate([m.imag, m.real], axis=-1)], axis=-2)


@functools.lru_cache(maxsize=None)
def _dft_tables(n, n1):
    n2 = n // n1
    h1 = n1 // 2
    ang = -2.0 * np.pi / n
    k1 = np.arange(n1)[None, :, None]
    nn1 = np.arange(n1)[None, None, :]
    nn2 = np.arange(n2)[:, None, None]
    g1_full = np.exp(1j * ang * k1 * (n2 * nn1 + nn2))
    g1 = _stack_complex(g1_full[:, :, :h1])
    g1_real = np.concatenate([g1_full.real, g1_full.imag], axis=-2)
    kk2 = np.arange(n2)[:, None]
    f2 = np.exp(-2j * np.pi * kk2 * np.arange(n2)[None, :] / n2)
    f2s = _stack_complex(f2)
    f3s = _stack_complex(np.conj(f2))
    out1 = np.arange(h1)[None, :, None]
    g4 = np.exp(-1j * ang * (n2 * out1 + nn2) * np.arange(n1)[None, None, :]) / n
    g4s = _stack_complex(g4)
    cast = lambda a: jnp.asarray(a, dtype=F32).astype(MXU_DTYPE)
    return cast(g1), cast(g1_real), cast(f2s), cast(f3s), cast(g4s)


def _hy_spectrum_kernel(filt_ref, g1r_ref, f2s_ref, hf_ref, a_ref, *, n1):
    n = filt_ref.shape[0]
    n2 = n // n1

    def s1(j, carry):
        x = filt_ref[pl.ds(j, n1, stride=n2), :].astype(MXU_DTYPE)
        out = jnp.dot(g1r_ref[j], x, preferred_element_type=F32)
        base = pl.multiple_of(j * n1, n1)
        a_ref[0, pl.ds(base, n1), :] = out[:n1]
        a_ref[1, pl.ds(base, n1), :] = out[n1:]
        return carry

    lax.fori_loop(0, n2, s1, 0)

    def s2(j, carry):
        x = jnp.concatenate([a_ref[0, pl.ds(j, n2, stride=n1), :],
                             a_ref[1, pl.ds(j, n2, stride=n1), :]], axis=0).astype(MXU_DTYPE)
        out = jnp.dot(f2s_ref[...], x, preferred_element_type=F32)
        base = pl.multiple_of(j * n2, n2)
        hf_ref[0, pl.ds(base, n2), :] = out[:n2]
        hf_ref[1, pl.ds(base, n2), :] = out[n2:]
        return carry

    lax.fori_loop(0, n1, s2, 0)


def hy_spectrum(filt):
    n, c = filt.shape
    n1 = FFT_N1
    _, g1r, f2s, _, _ = _dft_tables(n, n1)
    return pl.pallas_call(
        functools.partial(_hy_spectrum_kernel, n1=n1),
        grid=(c // LANES,),
        in_specs=[pl.BlockSpec((n, LANES), lambda j: (0, j)),
                  pl.BlockSpec(g1r.shape, lambda j: (0, 0, 0)),
                  pl.BlockSpec(f2s.shape, lambda j: (0, 0))],
        out_specs=pl.BlockSpec((2, n, LANES), lambda j: (0, 0, j)),
        out_shape=jax.ShapeDtypeStruct((2, n, c), F32),
        scratch_shapes=[pltpu.VMEM((2, n, LANES), F32)],
        compiler_params=_cparams(("parallel",)),
        name="hy_spectrum",
    )(filt, g1r, f2s)


def _hy_fftconv_kernel(v_ref, hf_ref, g1_ref, f2s_ref, f3s_ref, g4s_ref, o_ref, a_ref, y_ref, *, n1):
    seq = v_ref.shape[1]
    n = 2 * seq
    n2 = n // n1
    h1 = n1 // 2

    def s1(j, carry):
        x = jnp.concatenate([v_ref[0, pl.ds(j, h1, stride=n2), :],
                             v_ref[1, pl.ds(j, h1, stride=n2), :]], axis=0).astype(MXU_DTYPE)
        out = jnp.dot(g1_ref[j], x, preferred_element_type=F32)
        base = pl.multiple_of(j * n1, n1)
        a_ref[0, pl.ds(base, n1), :] = out[:n1]
        a_ref[1, pl.ds(base, n1), :] = out[n1:]
        return carry

    lax.fori_loop(0, n2, s1, 0)

    def s2(j, carry):
        x = jnp.concatenate([a_ref[0, pl.ds(j, n2, stride=n1), :],
                             a_ref[1, pl.ds(j, n2, stride=n1), :]], axis=0).astype(MXU_DTYPE)
        out = jnp.dot(f2s_ref[...], x, preferred_element_type=F32)
        base = pl.multiple_of(j * n2, n2)
        xr, xi = out[:n2], out[n2:]
        hr = hf_ref[0, pl.ds(base, n2), :]
        hi = hf_ref[1, pl.ds(base, n2), :]
        y_ref[0, pl.ds(base, n2), :] = xr * hr - xi * hi
        y_ref[1, pl.ds(base, n2), :] = xr * hi + xi * hr
        return carry

    lax.fori_loop(0, n1, s2, 0)

    def s3(j, carry):
        base = pl.multiple_of(j * n2, n2)
        x = jnp.concatenate([y_ref[0, pl.ds(base, n2), :],
                             y_ref[1, pl.ds(base, n2), :]], axis=0).astype(MXU_DTYPE)
        out = jnp.dot(f3s_ref[...], x, preferred_element_type=F32)
        a_ref[0, pl.ds(base, n2), :] = out[:n2]
        a_ref[1, pl.ds(base, n2), :] = out[n2:]
        return carry

    lax.fori_loop(0, n1, s3, 0)

    def s4(j, carry):
        x = jnp.concatenate([a_ref[0, pl.ds(j, n1, stride=n2), :],
                             a_ref[1, pl.ds(j, n1, stride=n2), :]], axis=0).astype(MXU_DTYPE)
        out = jnp.dot(g4s_ref[j], x, preferred_element_type=F32)
        o_ref[0, pl.ds(j, h1, stride=n2), :] = out[:h1]
        o_ref[1, pl.ds(j, h1, stride=n2), :] = out[h1:]
        return carry

    lax.fori_loop(0, n2, s4, 0)


def hy_fftconv(vx, hf):
    bsz, seq, c = vx.shape
    n = 2 * seq
    n1 = FFT_N1
    g1, _, f2s, f3s, g4s = _dft_tables(n, n1)
    once = pl.Buffered(1)
    tab3 = lambda a: pl.BlockSpec(a.shape, lambda j, p: (0, 0, 0), pipeline_mode=once)
    tab2 = lambda a: pl.BlockSpec(a.shape, lambda j, p: (0, 0), pipeline_mode=once)
    pair = pl.BlockSpec((None, 2, seq, LANES), lambda j, p: (p, 0, 0, j))
    out = pl.pallas_call(
        functools.partial(_hy_fftconv_kernel, n1=n1),
        grid=(c // LANES, bsz // 2),
        in_specs=[pair,
                  pl.BlockSpec((2, n, LANES), lambda j, p: (0, 0, j), pipeline_mode=once),
                  tab3(g1), tab2(f2s), tab2(f3s), tab3(g4s)],
        out_specs=pair,
        out_shape=jax.ShapeDtypeStruct((bsz // 2, 2, seq, c), F32),
        scratch_shapes=[pltpu.VMEM((2, n, LANES), F32), pltpu.VMEM((2, n, LANES), F32)],
        compiler_params=_cparams(("parallel", "parallel")),
        name="hy_fftconv",
    )(vx.reshape(bsz // 2, 2, seq, c), hf, g1, f2s, f3s, g4s)
    return out.reshape(bsz, seq, c)


def _hy_out_kernel(conv_ref, vx_ref, x0_ref, skip_ref, w_ref, b_ref, res_ref, o_ref):
    y = (conv_ref[...] + vx_ref[...] * skip_ref[...]) * x0_ref[...]
    o_ref[...] = res_ref[...] + _dot(y, w_ref[...]) + b_ref[...]


def hy_out(conv, vx, x0c, skip, out_w, out_b, res):
    t, c = vx.shape
    d = out_w.shape[1]
    tm = min(ROW_TILE, t)
    row = lambda i: (i, 0)
    const = lambda i: (0, 0)
    return pl.pallas_call(
        _hy_out_kernel,
        grid=(t // tm,),
        in_specs=[pl.BlockSpec((tm, c), row), pl.BlockSpec((tm, c), row), pl.BlockSpec((tm, c), row),
                  pl.BlockSpec((1, c), const), pl.BlockSpec((c, d), const), pl.BlockSpec((1, d), const),
                  pl.BlockSpec((tm, d), row)],
        out_specs=pl.BlockSpec((tm, d), row),
        out_shape=jax.ShapeDtypeStruct((t, d), F32),
        compiler_params=_cparams(("parallel",)),
        name="hy_out",
    )(conv, vx, x0c, skip, out_w, out_b, res)


def _pad_cols(a, width):
    return jnp.pad(a, ((0, 0), (0, width - a.shape[1])))


def _col_tile(n):
    for tile in (1024, 768, 512, 256, 128):
        if n % tile == 0:
            return tile
    raise ValueError(f"unsupported projection width {n}")


def mamba2_mixer(h, bsz, norm_w, in_w, conv_w, conv_b, dt_bias, a_log, d_skip, gn_w, out_w):
    t, d = h.shape
    seq = t // bsz
    heads = a_log.shape[1]
    d_inner = heads * SSD_HEADDIM
    conv_dim = conv_w.shape[1]
    main = d_inner + conv_dim
    zx = norm_linear(h, norm_w, in_w[:, :main].astype(MXU_DTYPE), jnp.zeros((1, main), F32), _col_tile(main))
    w_dt = _pad_cols(in_w[:, main:], LANES).astype(MXU_DTYPE)
    dt_raw = norm_linear(h, norm_w, w_dt, jnp.zeros((1, LANES), F32), LANES)
    xbc = ssd_conv(zx.reshape(bsz, seq, main), conv_w, conv_b[None, :], col_off=d_inner)
    y_f, y_b = ssd_scan(xbc, dt_raw.reshape(bsz, seq, LANES),
                        _pad_cols(dt_bias.reshape(1, 2 * heads), LANES),
                        _pad_cols(a_log.reshape(1, 2 * heads), LANES))
    return ssd_out(y_f.reshape(t, d_inner), y_b.reshape(t, d_inner), xbc.reshape(t, conv_dim), zx,
                   jnp.repeat(d_skip, SSD_HEADDIM)[None, :], gn_w[None, :], out_w.astype(MXU_DTYPE), h)


def hyena_mixer(h, bsz, norm_w, in_w, in_b, conv_w, conv_b, f_w1, f_b1, f_w2, f_b2, f_w3, sin_freq, skip,
                out_w, out_b):
    t, d = h.shape
    seq = t // bsz
    width = skip.shape[0]
    u = norm_linear(h, norm_w, in_w.astype(MXU_DTYPE), in_b[None, :], _col_tile(in_w.shape[1]))
    x0c, vx = hy_conv(u.reshape(bsz, seq, 3 * width), conv_w, conv_b[None, :])
    filt = hy_filter(seq, f_w1, f_b1[None, :], f_w2, f_b2[None, :], f_w3, sin_freq[None, :], width)
    conv = hy_fftconv(vx, hy_spectrum(filt))
    return hy_out(conv.reshape(t, width), vx.reshape(t, width), x0c.reshape(t, width), skip[None, :],
                  out_w.astype(MXU_DTYPE), out_b[None, :], h)


def gla_mixer(h, bsz, norm_w, in_w, gk_w, gk_b, hn_w, out_w):
    t, d = h.shape
    seq = t // bsz
    rank, key_dim = gk_w.shape[1], gk_w.shape[2]
    value_dim = out_w.shape[0]
    main = 2 * key_dim + 2 * value_dim
    qkvg = norm_linear(h, norm_w, in_w[:, :main].astype(MXU_DTYPE), jnp.zeros((1, main), F32), _col_tile(main))
    w_gl = _pad_cols(in_w[:, main:], LANES).astype(MXU_DTYPE)
    gl = norm_linear(h, norm_w, w_gl, jnp.zeros((1, LANES), F32), LANES)
    wgk = jnp.zeros((LANES, 2 * key_dim), F32)
    wgk = wgk.at[:rank, :key_dim].set(gk_w[0]).at[rank:2 * rank, key_dim:].set(gk_w[1]).astype(MXU_DTYPE)
    o_f, o_b = gla_scan(qkvg.reshape(bsz, seq, main), gl.reshape(bsz, seq, LANES), wgk,
                        gk_b.reshape(1, 2 * key_dim), key_dim, value_dim)
    return gla_out(o_f.reshape(t, value_dim), o_b.reshape(t, value_dim), qkvg, hn_w[None, :],
                   out_w.astype(MXU_DTYPE), h, g_col=2 * key_dim + value_dim)


def kernel(x, p, norm_mix, norm_ffn, norm_ple, ple_gate, ple_proj, ffn_w1, ffn_w3, ffn_w2, final_norm,
           ssd_in_w, ssd_conv_w, ssd_conv_b, ssd_dt_bias, ssd_a_log, ssd_d, ssd_norm, ssd_out_w,
           hy_in_w, hy_in_b, hy_conv_w, hy_conv_b, hy_f_w1, hy_f_b1, hy_f_w2, hy_f_b2, hy_f_w3,
           hy_sin_freq, hy_skip, hy_out_w, hy_out_b,
           gla_in_w, gla_gk_w, gla_gk_b, gla_norm, gla_out_w):
    bsz, seq, d = x.shape
    depth = p.shape[0]
    t = bsz * seq
    h = x.reshape(t, d)
    for i in range(depth):
        kind, j = i % N_MIXERS, i // N_MIXERS
        nw = norm_mix[i][None, :]
        if kind == 0:
            h = mamba2_mixer(h, bsz, nw, ssd_in_w[j], ssd_conv_w[j], ssd_conv_b[j], ssd_dt_bias[j], ssd_a_log[j],
                             ssd_d[j], ssd_norm[j], ssd_out_w[j])
        elif kind == 1:
            h = hyena_mixer(h, bsz, nw, hy_in_w[j], hy_in_b[j], hy_conv_w[j], hy_conv_b[j], hy_f_w1[j], hy_f_b1[j],
                            hy_f_w2[j], hy_f_b2[j], hy_f_w3[j], hy_sin_freq[j], hy_skip[j], hy_out_w[j], hy_out_b[j])
        else:
            h = gla_mixer(h, bsz, nw, gla_in_w[j], gla_gk_w[j], gla_gk_b[j], gla_norm[j], gla_out_w[j])
        h = ffn_ple(h, norm_ffn[i][None, :], ffn_w1[i].astype(MXU_DTYPE), ffn_w3[i].astype(MXU_DTYPE),
                    ffn_w2[i].astype(MXU_DTYPE), p[i].reshape(t, -1), norm_ple[i][None, :],
                    ple_gate[i].astype(MXU_DTYPE), ple_proj[i].astype(MXU_DTYPE), final_norm[None, :],
                    final=(i == depth - 1))
    return h.reshape(bsz, seq, d)
```

```python
import functools
import math

import numpy as np
import jax
import jax.numpy as jnp
from jax import lax
from jax.experimental import pallas as pl
from jax.experimental.pallas import tpu as pltpu

F32 = jnp.float32
MXU_DTYPE = jnp.bfloat16
EXACT = lax.Precision.HIGHEST

NORM_EPS = 1e-6
N_MIXERS = 3

VMEM_LIMIT_BYTES = 56 * 1024 * 1024
ROW_TILE = 512
LANES = 128

SSD_HEADDIM = 64
SSD_GROUPS = 8
SSD_STATE = 128
SSD_CHUNK = 128
SSD_PAIR = 2 * SSD_HEADDIM
assert SSD_PAIR == LANES == SSD_CHUNK
GLA_HEADS = 4
GLA_CHUNK = 64
GLA_BLOCK = 256
GLA_SUB = 16
GLA_GATE_RANK = 16
GLA_GATE_NORMALIZER = 16.0
HY_BANDS = 16
HY_DECAY_TARGET = 1e-2
HY_FAST_DECAY_PCT = 0.3
HY_SLOW_DECAY_PCT = 1.5
FFT_N1 = 64


def _cparams(sem):
    return pltpu.CompilerParams(dimension_semantics=sem, vmem_limit_bytes=VMEM_LIMIT_BYTES)


def _dot(a, b):
    return jnp.dot(a.astype(MXU_DTYPE), b.astype(MXU_DTYPE), preferred_element_type=F32)


def _dot_nt(a, b):
    return lax.dot_general(a.astype(MXU_DTYPE), b.astype(MXU_DTYPE), (((1,), (1,)), ((), ())),
                           preferred_element_type=F32)


def _dot_tn(a, b):
    return lax.dot_general(a.astype(MXU_DTYPE), b.astype(MXU_DTYPE), (((0,), (0,)), ((), ())),
                           preferred_element_type=F32)


def _dot_exact(a, b):
    return jnp.dot(a, b, preferred_element_type=F32, precision=EXACT)


def _split3(x):
    hi = x.astype(MXU_DTYPE)
    rest = x - hi.astype(F32)
    mid = rest.astype(MXU_DTYPE)
    lo = (rest - mid.astype(F32)).astype(MXU_DTYPE)
    return hi, mid, lo


def _mask_times(mask, x):
    m = jnp.where(mask, 1.0, 0.0).astype(MXU_DTYPE)
    return jnp.dot(jnp.concatenate([m, m, m], axis=1), jnp.concatenate(_split3(x), axis=0),
                   preferred_element_type=F32)


def _times_mask(x, mask):
    m = jnp.where(mask, 1.0, 0.0).astype(MXU_DTYPE)
    return jnp.dot(jnp.concatenate(_split3(x), axis=1), jnp.concatenate([m, m, m], axis=0),
                   preferred_element_type=F32)


def _rms(x, w):
    return x * lax.rsqrt(jnp.mean(x * x, axis=-1, keepdims=True) + NORM_EPS) * w


def _sigmoid(x):
    return 1.0 / (1.0 + jnp.exp(-x))


def _silu(x):
    return x * _sigmoid(x)


def _softplus(x):
    return jnp.maximum(x, 0.0) + jnp.log(1.0 + jnp.exp(-jnp.abs(x)))


def _norm_linear_kernel(x_ref, nw_ref, w_ref, b_ref, o_ref, xn_ref):
    @pl.when(pl.program_id(1) == 0)
    def _():
        xn_ref[...] = _rms(x_ref[...], nw_ref[...]).astype(xn_ref.dtype)

    o_ref[...] = jnp.dot(xn_ref[...], w_ref[...], preferred_element_type=F32) + b_ref[...]


def norm_linear(x, nw, w, b, col_tile):
    t, d = x.shape
    n = w.shape[1]
    tm = min(ROW_TILE, t)
    return pl.pallas_call(
        _norm_linear_kernel,
        grid=(t // tm, n // col_tile),
        in_specs=[pl.BlockSpec((tm, d), lambda i, j: (i, 0)),
                  pl.BlockSpec((1, d), lambda i, j: (0, 0)),
                  pl.BlockSpec((d, col_tile), lambda i, j: (0, j)),
                  pl.BlockSpec((1, col_tile), lambda i, j: (0, j))],
        out_specs=pl.BlockSpec((tm, col_tile), lambda i, j: (i, j)),
        out_shape=jax.ShapeDtypeStruct((t, n), F32),
        scratch_shapes=[pltpu.VMEM((tm, d), MXU_DTYPE)],
        compiler_params=_cparams(("parallel", "arbitrary")),
        name="norm_linear",
    )(x, nw, w, b)


def _ffn_ple_kernel(h_ref, nf_ref, w1_ref, w3_ref, w2_ref, p_ref, npl_ref, g_ref, pp_ref, fn_ref,
                    o_ref, xn_ref, acc_ref, *, final):
    f = pl.program_id(1)

    @pl.when(f == 0)
    def _():
        xn_ref[...] = _rms(h_ref[...], nf_ref[...]).astype(xn_ref.dtype)
        acc_ref[...] = jnp.zeros_like(acc_ref)

    xn = xn_ref[...]
    a = jnp.dot(xn, w1_ref[...], preferred_element_type=F32)
    b = jnp.dot(xn, w3_ref[...], preferred_element_type=F32)
    acc_ref[...] += _dot(_silu(a) * b, w2_ref[...])

    @pl.when(f == pl.num_programs(1) - 1)
    def _():
        h2 = h_ref[...] + acc_ref[...]
        gate = _sigmoid(_dot(_rms(h2, npl_ref[...]), g_ref[...]))
        h3 = h2 + gate * _dot(p_ref[...], pp_ref[...])
        if final:
            h3 = _rms(h3, fn_ref[...])
        o_ref[...] = h3


def ffn_ple(h, nf, w1, w3, w2, p, npl, g, pp, fn, final):
    t, d = h.shape
    dff = w1.shape[1]
    tf = dff // 2
    pdim = p.shape[1]
    tm = min(ROW_TILE, t)
    row = lambda i, f: (i, 0)
    const = lambda i, f: (0, 0)
    return pl.pallas_call(
        functools.partial(_ffn_ple_kernel, final=final),
        grid=(t // tm, dff // tf),
        in_specs=[pl.BlockSpec((tm, d), row),
                  pl.BlockSpec((1, d), const),
                  pl.BlockSpec((d, tf), lambda i, f: (0, f)),
                  pl.BlockSpec((d, tf), lambda i, f: (0, f)),
                  pl.BlockSpec((tf, d), lambda i, f: (f, 0)),
                  pl.BlockSpec((tm, pdim), row),
                  pl.BlockSpec((1, d), const),
                  pl.BlockSpec((d, d), const),
                  pl.BlockSpec((pdim, d), const),
                  pl.BlockSpec((1, d), const)],
        out_specs=pl.BlockSpec((tm, d), row),
        out_shape=jax.ShapeDtypeStruct((t, d), F32),
        scratch_shapes=[pltpu.VMEM((tm, d), MXU_DTYPE), pltpu.VMEM((tm, d), F32)],
        compiler_params=_cparams(("parallel", "arbitrary")),
        name="ffn_ple",
    )(h, nf, w1, w3, w2, p, npl, g, pp, fn)


CONV_PAD = 8
CONV_ROWS = 256


def _conv_taps(pad_ref, w_ref, b_ref, base, rows, taps):
    win = pad_ref[pl.ds(base, rows + 2 * CONV_PAD), :]
    acc = b_ref[...]
    for k in range(taps):
        lo = CONV_PAD + k - taps // 2
        acc = acc + w_ref[k:k + 1, :] * win[lo:lo + rows, :]
    return acc


def _fill_padded(pad_ref, x_ref, seq):
    zeros = jnp.zeros((CONV_PAD, pad_ref.shape[1]), F32)
    pad_ref[0:CONV_PAD, :] = zeros
    pad_ref[CONV_PAD + seq:2 * CONV_PAD + seq, :] = zeros
    pad_ref[CONV_PAD:CONV_PAD + seq, :] = x_ref[...]


def _ssd_conv_kernel(x_ref, w_ref, b_ref, o_ref, pad_ref, *, taps):
    seq = x_ref.shape[0]
    rows = min(CONV_ROWS, seq)
    _fill_padded(pad_ref, x_ref, seq)

    def body(r, carry):
        base = pl.multiple_of(r * rows, rows)
        o_ref[pl.ds(base, rows), :] = _silu(_conv_taps(pad_ref, w_ref, b_ref, base, rows, taps))
        return carry

    lax.fori_loop(0, seq // rows, body, 0)


def ssd_conv(zx, w, b, col_off, tc=256):
    bsz, seq, _ = zx.shape
    taps, c = w.shape
    off = col_off // tc
    return pl.pallas_call(
        functools.partial(_ssd_conv_kernel, taps=taps),
        grid=(bsz, c // tc),
        in_specs=[pl.BlockSpec((None, seq, tc), lambda i, j: (i, 0, j + off)),
                  pl.BlockSpec((taps, tc), lambda i, j: (0, j)),
                  pl.BlockSpec((1, tc), lambda i, j: (0, j))],
        out_specs=pl.BlockSpec((None, seq, tc), lambda i, j: (i, 0, j)),
        out_shape=jax.ShapeDtypeStruct((bsz, seq, c), F32),
        scratch_shapes=[pltpu.VMEM((seq + 2 * CONV_PAD, tc), F32)],
        compiler_params=_cparams(("parallel", "parallel")),
        name="ssd_conv",
    )(zx, w, b)


def _hy_conv_kernel(x0_ref, x1_ref, v_ref, w0_ref, w1_ref, wv_ref, b0_ref, b1_ref, bv_ref,
                    x0c_ref, vx_ref, p0_ref, p1_ref, pv_ref, *, taps):
    seq = x0_ref.shape[0]
    rows = min(CONV_ROWS, seq)
    _fill_padded(p0_ref, x0_ref, seq)
    _fill_padded(p1_ref, x1_ref, seq)
    _fill_padded(pv_ref, v_ref, seq)

    def body(r, carry):
        base = pl.multiple_of(r * rows, rows)
        x0c_ref[pl.ds(base, rows), :] = _conv_taps(p0_ref, w0_ref, b0_ref, base, rows, taps)
        vx_ref[pl.ds(base, rows), :] = (_conv_taps(pv_ref, wv_ref, bv_ref, base, rows, taps)
                                        * _conv_taps(p1_ref, w1_ref, b1_ref, base, rows, taps))
        return carry

    lax.fori_loop(0, seq // rows, body, 0)


def hy_conv(u, w, b, tc=256):
    bsz, seq, c3 = u.shape
    c = c3 // 3
    taps = w.shape[0]
    nb = c // tc
    xspec = lambda part: pl.BlockSpec((None, seq, tc), lambda i, j: (i, 0, j + part * nb))
    wspec = lambda part: pl.BlockSpec((taps, tc), lambda i, j: (0, j + part * nb))
    bspec = lambda part: pl.BlockSpec((1, tc), lambda i, j: (0, j + part * nb))
    ospec = pl.BlockSpec((None, seq, tc), lambda i, j: (i, 0, j))
    pad = pltpu.VMEM((seq + 2 * CONV_PAD, tc), F32)
    return pl.pallas_call(
        functools.partial(_hy_conv_kernel, taps=taps),
        grid=(bsz, nb),
        in_specs=[xspec(0), xspec(1), xspec(2), wspec(0), wspec(1), wspec(2), bspec(0), bspec(1), bspec(2)],
        out_specs=[ospec, ospec],
        out_shape=[jax.ShapeDtypeStruct((bsz, seq, c), F32)] * 2,
        scratch_shapes=[pad, pad, pad],
        compiler_params=_cparams(("parallel", "parallel")),
        name="hy_conv",
    )(u, u, u, w, w, w, b, b, b)


def _tri(n, reverse):
    row = lax.broadcasted_iota(jnp.int32, (n, n), 0)
    col = lax.broadcasted_iota(jnp.int32, (n, n), 1)
    return (row <= col) if reverse else (row >= col)


def _ssd_direction(xs_ref, b_ref, c_ref, dt_ref, dtb_ref, alog_ref, st_ref, y_ref, *, lane0, reverse):
    t = SSD_CHUNK
    heads = xs_ref.shape[1] // SSD_HEADDIM
    per_group = heads // SSD_GROUPS
    gw = per_group * SSD_HEADDIM
    tri = _tri(t, reverse)

    dt_all = _softplus(dt_ref[...] + dtb_ref[...])
    da_all = dt_all * (-jnp.exp(alog_ref[...]))
    dt = dt_all[:, lane0:lane0 + heads]
    da = da_all[:, lane0:lane0 + heads]
    dt_t = dt_all.T[lane0:lane0 + heads, :]
    da_t = da_all.T[lane0:lane0 + heads, :]
    acum = _mask_times(tri, da)
    acum_t = _times_mask(da_t, _tri(t, not reverse))
    tot = jnp.sum(da, axis=0, keepdims=True)
    w = jnp.exp(tot - acum) * dt
    etot = jnp.exp(tot)

    xs = xs_ref[...]
    first = lax.broadcasted_iota(jnp.int32, (t, SSD_PAIR), 1) < SSD_HEADDIM
    for g in range(SSD_GROUPS):
        bg_t = b_ref[:, g * SSD_STATE:(g + 1) * SSD_STATE].T.astype(MXU_DTYPE)
        cg = c_ref[:, g * SSD_STATE:(g + 1) * SSD_STATE].astype(MXU_DTYPE)
        cb = jnp.dot(cg, bg_t, preferred_element_type=F32)
        st = st_ref[g]
        gl = slice(g * gw, (g + 1) * gw)
        y_intra, e_acum, e_tot, xw = [], [], [], []
        for pr in range(per_group // 2):
            h0 = g * per_group + 2 * pr
            x_pair = xs[:, h0 * SSD_HEADDIM:(h0 + 2) * SSD_HEADDIM]
            cols = [jnp.broadcast_to(acum[:, h:h + 1], (t, SSD_PAIR)) for h in (h0, h0 + 1)]
            scores = [cb * jnp.exp(jnp.where(tri, cols[k] - acum_t[h0 + k:h0 + k + 1, :], -jnp.inf))
                      * dt_t[h0 + k:h0 + k + 1, :] for k in range(2)]
            rhs = jnp.concatenate([jnp.where(first, x_pair, 0.0), jnp.where(first, 0.0, x_pair)], axis=0)
            y_intra.append(_dot(jnp.concatenate(scores, axis=1), rhs))
            e_acum.append(jnp.where(first, jnp.exp(cols[0]), jnp.exp(cols[1])))
            e_tot.append(jnp.where(first[0:1], jnp.broadcast_to(etot[:, h0:h0 + 1], (1, SSD_PAIR)),
                                   jnp.broadcast_to(etot[:, h0 + 1:h0 + 2], (1, SSD_PAIR))))
            xw.append(x_pair * jnp.where(first, jnp.broadcast_to(w[:, h0:h0 + 1], (t, SSD_PAIR)),
                                         jnp.broadcast_to(w[:, h0 + 1:h0 + 2], (t, SSD_PAIR))))
        y_ref[:, gl] = _dot(cg, st) * jnp.concatenate(e_acum, axis=1) + jnp.concatenate(y_intra, axis=1)
        st_ref[g] = (st * jnp.concatenate(e_tot, axis=1)
                     + jnp.dot(bg_t, jnp.concatenate(xw, axis=1).astype(MXU_DTYPE), preferred_element_type=F32))


def _ssd_scan_kernel(xf_ref, bf_ref, cf_ref, dtf_ref, xb_ref, bb_ref, cb_ref, dtb_ref,
                     bias_ref, alog_ref, yf_ref, yb_ref, stf_ref, stb_ref):
    @pl.when(pl.program_id(1) == 0)
    def _():
        stf_ref[...] = jnp.zeros_like(stf_ref)
        stb_ref[...] = jnp.zeros_like(stb_ref)

    heads = xf_ref.shape[1] // SSD_HEADDIM
    _ssd_direction(xf_ref, bf_ref, cf_ref, dtf_ref, bias_ref, alog_ref, stf_ref, yf_ref, lane0=0, reverse=False)
    _ssd_direction(xb_ref, bb_ref, cb_ref, dtb_ref, bias_ref, alog_ref, stb_ref, yb_ref, lane0=heads, reverse=True)


def ssd_scan(xbc, dt_raw, dt_bias, a_log):
    bsz, seq, width = xbc.shape
    gn = SSD_GROUPS * SSD_STATE
    d_inner = width - 2 * gn
    t = SSD_CHUNK
    nc = seq // t
    xblk = d_inner // gn
    fwd = lambda b, c: (b, c)
    bwd = lambda b, c: (b, nc - 1 - c)

    def specs(pos):
        return [pl.BlockSpec((None, t, d_inner), lambda b, c: (*pos(b, c), 0)),
                pl.BlockSpec((None, t, gn), lambda b, c: (*pos(b, c), xblk)),
                pl.BlockSpec((None, t, gn), lambda b, c: (*pos(b, c), xblk + 1)),
                pl.BlockSpec((None, t, LANES), lambda b, c: (*pos(b, c), 0))]

    const = pl.BlockSpec((1, LANES), lambda b, c: (0, 0))
    state = pltpu.VMEM((SSD_GROUPS, SSD_STATE, d_inner // SSD_GROUPS), F32)
    return pl.pallas_call(
        _ssd_scan_kernel,
        grid=(bsz, nc),
        in_specs=specs(fwd) + specs(bwd) + [const, const],
        out_specs=[pl.BlockSpec((None, t, d_inner), lambda b, c: (*fwd(b, c), 0)),
                   pl.BlockSpec((None, t, d_inner), lambda b, c: (*bwd(b, c), 0))],
        out_shape=[jax.ShapeDtypeStruct((bsz, seq, d_inner), F32)] * 2,
        scratch_shapes=[state, state],
        compiler_params=_cparams(("parallel", "arbitrary")),
        name="ssd_scan",
    )(xbc, xbc, xbc, dt_raw, xbc, xbc, xbc, dt_raw, dt_bias, a_log)


def _ssd_out_kernel(yf_ref, yb_ref, xs_ref, z_ref, dsk_ref, nw_ref, w_ref, res_ref, o_ref):
    y = (yf_ref[...] + yb_ref[...] + xs_ref[...] * dsk_ref[...]) * _silu(z_ref[...])
    gw = y.shape[1] // SSD_GROUPS
    y = jnp.concatenate([_rms(y[:, g * gw:(g + 1) * gw], nw_ref[:, g * gw:(g + 1) * gw])
                         for g in range(SSD_GROUPS)], axis=1)
    o_ref[...] = res_ref[...] + _dot(y, w_ref[...])


def ssd_out(y_f, y_b, xbc, zx, d_skip, norm_w, out_w, res):
    t, d_inner = y_f.shape
    d = out_w.shape[1]
    tm = min(ROW_TILE, t)
    row = lambda i: (i, 0)
    const = lambda i: (0, 0)
    return pl.pallas_call(
        _ssd_out_kernel,
        grid=(t // tm,),
        in_specs=[pl.BlockSpec((tm, d_inner), row), pl.BlockSpec((tm, d_inner), row),
                  pl.BlockSpec((tm, d_inner), row), pl.BlockSpec((tm, d_inner), row),
                  pl.BlockSpec((1, d_inner), const), pl.BlockSpec((1, d_inner), const),
                  pl.BlockSpec((d_inner, d), const), pl.BlockSpec((tm, d), row)],
        out_specs=pl.BlockSpec((tm, d), row),
        out_shape=jax.ShapeDtypeStruct((t, d), F32),
        compiler_params=_cparams(("parallel",)),
        name="ssd_out",
    )(y_f, y_b, xbc, zx, d_skip, norm_w, out_w, res)


def _pad_rows(x, r0, total):
    parts = []
    if r0 > 0:
        parts.append(jnp.zeros((r0, x.shape[1]), x.dtype))
    parts.append(x)
    if r0 + x.shape[0] < total:
        parts.append(jnp.zeros((total - r0 - x.shape[0], x.shape[1]), x.dtype))
    return jnp.concatenate(parts, axis=0) if len(parts) > 1 else x


def _gla_chunk(q_ref, k_ref, v_ref, o_ref, g, states, rs, *, reverse):
    ch = GLA_CHUNK
    kd = q_ref.shape[1]
    dk = kd // GLA_HEADS
    dv = v_ref.shape[1] // GLA_HEADS
    nsub = ch // GLA_SUB
    tri = _tri(ch, reverse)
    eye = lax.broadcasted_iota(jnp.int32, (dk, dk), 0) == lax.broadcasted_iota(jnp.int32, (dk, dk), 1)
    bc = _mask_times(tri, g)
    tot = jnp.sum(g, axis=0, keepdims=True)
    q = q_ref[rs, :] * (dk ** -0.5)
    k = k_ref[rs, :]
    v = v_ref[rs, :].astype(MXU_DTYPE)
    q_in = q * jnp.exp(bc)
    k_st = k * jnp.exp(tot - bc)
    e_tot = jnp.exp(tot)
    new_states = []
    for h in range(GLA_HEADS):
        ks = slice(h * dk, (h + 1) * dk)
        vs = slice(h * dv, (h + 1) * dv)
        bch, qh, kh = bc[:, ks], q[:, ks], k[:, ks]
        q_segs, k_segs = [], []
        for i in range(nsub):
            l0, l1 = i * GLA_SUB, (i + 1) * GLA_SUB
            if reverse:
                s0, s1 = l0, ch
                ref = bch[l1:l1 + 1] if l1 < ch else 0.0
            else:
                s0, s1 = 0, l1
                ref = bch[l0 - 1:l0] if l0 > 0 else 0.0
            q_segs.append(_pad_rows(qh[l0:l1] * jnp.exp(bch[l0:l1] - ref), l0, ch))
            k_segs.append(_pad_rows(kh[s0:s1] * jnp.exp(ref - bch[s0:s1]), s0, ch))
        att = _dot_nt(jnp.concatenate(q_segs, axis=1), jnp.concatenate(k_segs, axis=1))
        att = jnp.where(tri, att, 0.0)
        st = states[h]
        o_ref[rs, vs] = _dot(jnp.concatenate([q_in[:, ks], att], axis=1),
                             jnp.concatenate([st.astype(MXU_DTYPE), v[:, vs]], axis=0))
        e_col = jnp.sum(jnp.where(eye, jnp.broadcast_to(e_tot[:, ks], (dk, dk)), 0.0), axis=1, keepdims=True)
        new_states.append(st * e_col + _dot_tn(k_st[:, ks], v[:, vs]))
    return new_states


def _gla_gate(gl_ref, wgk_ref, bgk_ref, cols):
    x = _dot(gl_ref[...], wgk_ref[:, cols]) + bgk_ref[:, cols]
    return -_softplus(-x) * (1.0 / GLA_GATE_NORMALIZER)


def _gla_scan_kernel(qf_ref, kf_ref, vf_ref, glf_ref, qb_ref, kb_ref, vb_ref, glb_ref, wgk_ref, bgk_ref,
                     of_ref, ob_ref, stf_ref, stb_ref):
    @pl.when(pl.program_id(1) == 0)
    def _():
        stf_ref[...] = jnp.zeros_like(stf_ref)
        stb_ref[...] = jnp.zeros_like(stb_ref)

    ch = GLA_CHUNK
    rows, kd = qf_ref.shape
    nchunks = rows // ch
    g_f = _gla_gate(glf_ref, wgk_ref, bgk_ref, slice(0, kd))
    g_b = _gla_gate(glb_ref, wgk_ref, bgk_ref, slice(kd, 2 * kd))
    st_f = [stf_ref[h] for h in range(GLA_HEADS)]
    st_b = [stb_ref[h] for h in range(GLA_HEADS)]
    for i in range(nchunks):
        rf = slice(i * ch, (i + 1) * ch)
        rb = slice((nchunks - 1 - i) * ch, (nchunks - i) * ch)
        st_f = _gla_chunk(qf_ref, kf_ref, vf_ref, of_ref, g_f[rf], st_f, rf, reverse=False)
        st_b = _gla_chunk(qb_ref, kb_ref, vb_ref, ob_ref, g_b[rb], st_b, rb, reverse=True)
    for h in range(GLA_HEADS):
        stf_ref[h] = st_f[h]
        stb_ref[h] = st_b[h]


def gla_scan(qkvg, gl, wgk, bgk, key_dim, value_dim):
    bsz, seq, _ = qkvg.shape
    ch = min(GLA_BLOCK, seq)
    nc = seq // ch
    fwd = lambda b, c: (b, c)
    bwd = lambda b, c: (b, nc - 1 - c)

    def specs(pos):
        return [pl.BlockSpec((None, ch, key_dim), lambda b, c: (*pos(b, c), 0)),
                pl.BlockSpec((None, ch, key_dim), lambda b, c: (*pos(b, c), 1)),
                pl.BlockSpec((None, ch, value_dim), lambda b, c: (*pos(b, c), (2 * key_dim) // value_dim)),
                pl.BlockSpec((None, ch, LANES), lambda b, c: (*pos(b, c), 0))]

    state = pltpu.VMEM((GLA_HEADS, key_dim // GLA_HEADS, value_dim // GLA_HEADS), F32)
    return pl.pallas_call(
        _gla_scan_kernel,
        grid=(bsz, nc),
        in_specs=specs(fwd) + specs(bwd) + [pl.BlockSpec((LANES, 2 * key_dim), lambda b, c: (0, 0)),
                                            pl.BlockSpec((1, 2 * key_dim), lambda b, c: (0, 0))],
        out_specs=[pl.BlockSpec((None, ch, value_dim), lambda b, c: (*fwd(b, c), 0)),
                   pl.BlockSpec((None, ch, value_dim), lambda b, c: (*bwd(b, c), 0))],
        out_shape=[jax.ShapeDtypeStruct((bsz, seq, value_dim), F32)] * 2,
        scratch_shapes=[state, state],
        compiler_params=_cparams(("parallel", "arbitrary")),
        name="gla_scan",
    )(qkvg, qkvg, qkvg, gl, qkvg, qkvg, qkvg, gl, wgk, bgk)


def _gla_out_kernel(of_ref, ob_ref, g_ref, nw_ref, w_ref, res_ref, o_ref):
    o = of_ref[...] + ob_ref[...]
    dv = o.shape[1] // GLA_HEADS
    o = jnp.concatenate([_rms(o[:, h * dv:(h + 1) * dv], nw_ref[...]) for h in range(GLA_HEADS)], axis=1)
    o_ref[...] = res_ref[...] + _dot(o * _silu(g_ref[...]), w_ref[...])


def gla_out(o_f, o_b, qkvg, norm_w, out_w, res, g_col):
    t, vd = o_f.shape
    d = out_w.shape[1]
    tm = min(ROW_TILE, t)
    row = lambda i: (i, 0)
    const = lambda i: (0, 0)
    return pl.pallas_call(
        _gla_out_kernel,
        grid=(t // tm,),
        in_specs=[pl.BlockSpec((tm, vd), row), pl.BlockSpec((tm, vd), row),
                  pl.BlockSpec((tm, vd), lambda i: (i, g_col // vd)),
                  pl.BlockSpec((1, vd // GLA_HEADS), const),
                  pl.BlockSpec((vd, d), const), pl.BlockSpec((tm, d), row)],
        out_specs=pl.BlockSpec((tm, d), row),
        out_shape=jax.ShapeDtypeStruct((t, d), F32),
        compiler_params=_cparams(("parallel",)),
        name="gla_out",
    )(o_f, o_b, qkvg, norm_w, out_w, res)


def _hy_filter_kernel(w1_ref, b1_ref, w2_ref, b2_ref, w3_ref, freq_ref, delta_ref, o_ref, *, seq):
    tr = o_ref.shape[0]
    r = pl.program_id(1) * tr + lax.broadcasted_iota(jnp.int32, (tr, 1), 0)
    pos = jnp.where(r < seq, r, 2 * seq - r).astype(F32)
    t = pos / (seq - 1.0)
    w = (2.0 * math.pi) * pos / seq
    band = lax.broadcasted_iota(jnp.int32, (1, HY_BANDS), 1).astype(F32)
    bands = 1e-4 + band * ((HY_BANDS - 1 - 1e-4) / (HY_BANDS - 1))
    zw1 = (t * w1_ref[0:1, :] + _dot_exact(jnp.cos(bands * w), w1_ref[1:1 + HY_BANDS, :])
           + _dot_exact(-jnp.sin(bands * w), w1_ref[1 + HY_BANDS:1 + 2 * HY_BANDS, :]))
    freq = freq_ref[...]
    hid = jnp.sin(freq * (zw1 + b1_ref[...]))
    hid = jnp.sin(freq * (_dot_exact(hid, w2_ref[...]) + b2_ref[...]))
    filt = _dot_exact(hid, w3_ref[...]) * jnp.exp(-t * delta_ref[...])
    o_ref[...] = jnp.where(r == seq, 0.0, filt)


def hy_filter(seq, w1, b1, w2, b2, w3, freq, width, tr=512):
    n = 2 * seq
    tr = min(tr, seq)
    tc = 512
    emb, hidden = w1.shape
    min_decay = math.log(HY_DECAY_TARGET) / HY_SLOW_DECAY_PCT
    max_decay = math.log(HY_DECAY_TARGET) / HY_FAST_DECAY_PCT
    deltas = jnp.abs(jnp.linspace(min_decay, max_decay, width, dtype=F32))[None, :]
    nct = width // tc
    half = seq // tr
    const = lambda j, i: (0, 0)
    return pl.pallas_call(
        functools.partial(_hy_filter_kernel, seq=seq),
        grid=(nct, n // tr),
        in_specs=[pl.BlockSpec((emb, hidden), const), pl.BlockSpec((1, hidden), const),
                  pl.BlockSpec((hidden, hidden), const), pl.BlockSpec((1, hidden), const),
                  pl.BlockSpec((hidden, tc), lambda j, i: (0, j + nct * (i // half))),
                  pl.BlockSpec((1, hidden), const),
                  pl.BlockSpec((1, tc), lambda j, i: (0, j))],
        out_specs=pl.BlockSpec((tr, tc), lambda j, i: (i, j)),
        out_shape=jax.ShapeDtypeStruct((n, width), F32),
        compiler_params=_cparams(("parallel", "parallel")),
        name="hy_filter",
    )(w1, b1, w2, b2, w3, freq, deltas)


def _stack_complex(m):
    return np.concatenate([np.concatenate([m.real, -m.imag], axis=-1),
                           np.concatenate([m.imag, m.real], axis=-1)], axis=-2)


@functools.lru_cache(maxsize=None)
def _dft_tables(n, n1):
    n2 = n // n1
    h1 = n1 // 2
    ang = -2.0 * np.pi / n
    k1 = np.arange(n1)[None, :, None]
    nn1 = np.arange(n1)[None, None, :]
    nn2 = np.arange(n2)[:, None, None]
    g1_full = np.exp(1j * ang * k1 * (n2 * nn1 + nn2))
    g1 = _stack_complex(g1_full[:, :, :h1])
    g1_real = np.concatenate([g1_full.real, g1_full.imag], axis=-2)
    kk2 = np.arange(n2)[:, None]
    f2 = np.exp(-2j * np.pi * kk2 * np.arange(n2)[None, :] / n2)
    f2s = _stack_complex(f2)
    f3s = _stack_complex(np.conj(f2))
    out1 = np.arange(h1)[None, :, None]
    g4 = np.exp(-1j * ang * (n2 * out1 + nn2) * np.arange(n1)[None, None, :]) / n
    g4s = _stack_complex(g4)
    cast = lambda a: jnp.asarray(a, dtype=F32).astype(MXU_DTYPE)
    return cast(g1), cast(g1_real), cast(f2s), cast(f3s), cast(g4s)


FFT_UNROLL = 8
FFT_SLABS = 4
FFT_PAD = 8


def _load_complex_strided(ref, start, rows, stride):
    return jnp.concatenate([ref[0, pl.ds(start, rows, stride=stride), :],
                            ref[1, pl.ds(start, rows, stride=stride), :]], axis=0)


def _store_complex_slab(ref, base, rows, out):
    ref[0, pl.ds(base, rows), :] = out[:rows]
    ref[1, pl.ds(base, rows), :] = out[rows:]


def _shared_matrix_stage(load, store, mat_ref, count):
    def body(jj, carry):
        j0 = jj * FFT_SLABS
        x = jnp.concatenate([load(j0 + u) for u in range(FFT_SLABS)], axis=1).astype(MXU_DTYPE)
        out = jnp.dot(mat_ref[...], x, preferred_element_type=F32)
        for u in range(FFT_SLABS):
            store(j0 + u, out[:, u * LANES:(u + 1) * LANES])
        return carry

    lax.fori_loop(0, count // FFT_SLABS, body, 0, unroll=2)


def _per_slab_matrix_stage(load, store, mats_ref, count):
    def body(j, carry):
        store(j, jnp.dot(mats_ref[j], load(j).astype(MXU_DTYPE), preferred_element_type=F32))
        return carry

    lax.fori_loop(0, count, body, 0, unroll=FFT_UNROLL)


def _slab_base(j, rows):
    return pl.multiple_of(j * (rows + FFT_PAD), 8)


def _scratch_rows(n, n1):
    n2 = n // n1
    return max(n2 * (n1 + FFT_PAD), n1 * (n2 + FFT_PAD))


def _hy_spectrum_kernel(filt_ref, g1r_ref, f2s_ref, hf_ref, a_ref, *, n1):
    n = filt_ref.shape[0]
    n2 = n // n1
    _per_slab_matrix_stage(
        lambda j: filt_ref[pl.ds(j, n1, stride=n2), :],
        lambda j, out: _store_complex_slab(a_ref, _slab_base(j, n1), n1, out),
        g1r_ref, n2)
    _shared_matrix_stage(
        lambda j: _load_complex_strided(a_ref, j, n2, n1 + FFT_PAD),
        lambda j, out: _store_complex_slab(hf_ref, pl.multiple_of(j * n2, n2), n2, out),
        f2s_ref, n1)


def hy_spectrum(filt):
    n, c = filt.shape
    n1 = FFT_N1
    _, g1r, f2s, _, _ = _dft_tables(n, n1)
    return pl.pallas_call(
        functools.partial(_hy_spectrum_kernel, n1=n1),
        grid=(c // LANES,),
        in_specs=[pl.BlockSpec((n, LANES), lambda j: (0, j)),
                  pl.BlockSpec(g1r.shape, lambda j: (0, 0, 0)),
                  pl.BlockSpec(f2s.shape, lambda j: (0, 0))],
        out_specs=pl.BlockSpec((2, n, LANES), lambda j: (0, 0, j)),
        out_shape=jax.ShapeDtypeStruct((2, n, c), F32),
        scratch_shapes=[pltpu.VMEM((2, _scratch_rows(n, n1), LANES), F32)],
        compiler_params=_cparams(("parallel",)),
        name="hy_spectrum",
    )(filt, g1r, f2s)


def _hy_fftconv_kernel(v_ref, hf_ref, g1_ref, f2s_ref, f3s_ref, g4s_ref, o_ref, a_ref, y_ref, *, n1):
    seq = v_ref.shape[1]
    n = 2 * seq
    n2 = n // n1
    h1 = n1 // 2

    def multiply_by_spectrum(j, out):
        base = pl.multiple_of(j * n2, n2)
        xr, xi = out[:n2], out[n2:]
        hr = hf_ref[0, pl.ds(base, n2), :]
        hi = hf_ref[1, pl.ds(base, n2), :]
        y_ref[0, pl.ds(base, n2), :] = xr * hr - xi * hi
        y_ref[1, pl.ds(base, n2), :] = xr * hi + xi * hr

    def store_outputs(j, out):
        o_ref[0, pl.ds(j, h1, stride=n2), :] = out[:h1]
        o_ref[1, pl.ds(j, h1, stride=n2), :] = out[h1:]

    _per_slab_matrix_stage(
        lambda j: _load_complex_strided(v_ref, j, h1, n2),
        lambda j, out: _store_complex_slab(a_ref, _slab_base(j, n1), n1, out),
        g1_ref, n2)
    _shared_matrix_stage(lambda j: _load_complex_strided(a_ref, j, n2, n1 + FFT_PAD), multiply_by_spectrum,
                         f2s_ref, n1)
    _shared_matrix_stage(
        lambda j: jnp.concatenate([y_ref[0, pl.ds(pl.multiple_of(j * n2, n2), n2), :],
                                   y_ref[1, pl.ds(pl.multiple_of(j * n2, n2), n2), :]], axis=0),
        lambda j, out: _store_complex_slab(a_ref, _slab_base(j, n2), n2, out),
        f3s_ref, n1)
    _per_slab_matrix_stage(lambda j: _load_complex_strided(a_ref, j, n1, n2 + FFT_PAD), store_outputs, g4s_ref, n2)


def hy_fftconv(vx, hf):
    bsz, seq, c = vx.shape
    n = 2 * seq
    n1 = FFT_N1
    g1, _, f2s, f3s, g4s = _dft_tables(n, n1)
    once = pl.Buffered(1)
    tab3 = lambda a: pl.BlockSpec(a.shape, lambda j, p: (0, 0, 0), pipeline_mode=once)
    tab2 = lambda a: pl.BlockSpec(a.shape, lambda j, p: (0, 0), pipeline_mode=once)
    pair = pl.BlockSpec((None, 2, seq, LANES), lambda j, p: (p, 0, 0, j))
    out = pl.pallas_call(
        functools.partial(_hy_fftconv_kernel, n1=n1),
        grid=(c // LANES, bsz // 2),
        in_specs=[pair,
                  pl.BlockSpec((2, n, LANES), lambda j, p: (0, 0, j), pipeline_mode=once),
                  tab3(g1), tab2(f2s), tab2(f3s), tab3(g4s)],
        out_specs=pair,
        out_shape=jax.ShapeDtypeStruct((bsz // 2, 2, seq, c), F32),
        scratch_shapes=[pltpu.VMEM((2, _scratch_rows(n, n1), LANES), F32), pltpu.VMEM((2, n, LANES), F32)],
        compiler_params=_cparams(("parallel", "parallel")),
        name="hy_fftconv",
    )(vx.reshape(bsz // 2, 2, seq, c), hf, g1, f2s, f3s, g4s)
    return out.reshape(bsz, seq, c)


def _hy_out_kernel(conv_ref, vx_ref, x0_ref, skip_ref, w_ref, b_ref, res_ref, o_ref):
    y = (conv_ref[...] + vx_ref[...] * skip_ref[...]) * x0_ref[...]
    o_ref[...] = res_ref[...] + _dot(y, w_ref[...]) + b_ref[...]


def hy_out(conv, vx, x0c, skip, out_w, out_b, res):
    t, c = vx.shape
    d = out_w.shape[1]
    tm = min(ROW_TILE, t)
    row = lambda i: (i, 0)
    const = lambda i: (0, 0)
    return pl.pallas_call(
        _hy_out_kernel,
        grid=(t // tm,),
        in_specs=[pl.BlockSpec((tm, c), row), pl.BlockSpec((tm, c), row), pl.BlockSpec((tm, c), row),
                  pl.BlockSpec((1, c), const), pl.BlockSpec((c, d), const), pl.BlockSpec((1, d), const),
                  pl.BlockSpec((tm, d), row)],
        out_specs=pl.BlockSpec((tm, d), row),
        out_shape=jax.ShapeDtypeStruct((t, d), F32),
        compiler_params=_cparams(("parallel",)),
        name="hy_out",
    )(conv, vx, x0c, skip, out_w, out_b, res)


def _pad_cols(a, width):
    return jnp.pad(a, ((0, 0), (0, width - a.shape[1])))


def _col_tile(n):
    for tile in (1024, 768, 512, 256, 128):
        if n % tile == 0:
            return tile
    raise ValueError(f"unsupported projection width {n}")


def mamba2_mixer(h, bsz, norm_w, in_w, conv_w, conv_b, dt_bias, a_log, d_skip, gn_w, out_w):
    t, d = h.shape
    seq = t // bsz
    heads = a_log.shape[1]
    d_inner = heads * SSD_HEADDIM
    conv_dim = conv_w.shape[1]
    main = d_inner + conv_dim
    zx = norm_linear(h, norm_w, in_w[:, :main].astype(MXU_DTYPE), jnp.zeros((1, main), F32), _col_tile(main))
    w_dt = _pad_cols(in_w[:, main:], LANES).astype(MXU_DTYPE)
    dt_raw = norm_linear(h, norm_w, w_dt, jnp.zeros((1, LANES), F32), LANES)
    xbc = ssd_conv(zx.reshape(bsz, seq, main), conv_w, conv_b[None, :], col_off=d_inner)
    y_f, y_b = ssd_scan(xbc, dt_raw.reshape(bsz, seq, LANES),
                        _pad_cols(dt_bias.reshape(1, 2 * heads), LANES),
                        _pad_cols(a_log.reshape(1, 2 * heads), LANES))
    return ssd_out(y_f.reshape(t, d_inner), y_b.reshape(t, d_inner), xbc.reshape(t, conv_dim), zx,
                   jnp.repeat(d_skip, SSD_HEADDIM)[None, :], gn_w[None, :], out_w.astype(MXU_DTYPE), h)


def hyena_mixer(h, bsz, norm_w, in_w, in_b, conv_w, conv_b, f_w1, f_b1, f_w2, f_b2, f_w3, sin_freq, skip,
                out_w, out_b):
    t, d = h.shape
    seq = t // bsz
    width = skip.shape[0]
    u = norm_linear(h, norm_w, in_w.astype(MXU_DTYPE), in_b[None, :], _col_tile(in_w.shape[1]))
    x0c, vx = hy_conv(u.reshape(bsz, seq, 3 * width), conv_w, conv_b[None, :])
    filt = hy_filter(seq, f_w1, f_b1[None, :], f_w2, f_b2[None, :], f_w3, sin_freq[None, :], width)
    conv = hy_fftconv(vx, hy_spectrum(filt))
    return hy_out(conv.reshape(t, width), vx.reshape(t, width), x0c.reshape(t, width), skip[None, :],
                  out_w.astype(MXU_DTYPE), out_b[None, :], h)


def gla_mixer(h, bsz, norm_w, in_w, gk_w, gk_b, hn_w, out_w):
    t, d = h.shape
    seq = t // bsz
    rank, key_dim = gk_w.shape[1], gk_w.shape[2]
    value_dim = out_w.shape[0]
    main = 2 * key_dim + 2 * value_dim
    qkvg = norm_linear(h, norm_w, in_w[:, :main].astype(MXU_DTYPE), jnp.zeros((1, main), F32), _col_tile(main))
    w_gl = _pad_cols(in_w[:, main:], LANES).astype(MXU_DTYPE)
    gl = norm_linear(h, norm_w, w_gl, jnp.zeros((1, LANES), F32), LANES)
    wgk = jnp.zeros((LANES, 2 * key_dim), F32)
    wgk = wgk.at[:rank, :key_dim].set(gk_w[0]).at[rank:2 * rank, key_dim:].set(gk_w[1]).astype(MXU_DTYPE)
    o_f, o_b = gla_scan(qkvg.reshape(bsz, seq, main), gl.reshape(bsz, seq, LANES), wgk,
                        gk_b.reshape(1, 2 * key_dim), key_dim, value_dim)
    return gla_out(o_f.reshape(t, value_dim), o_b.reshape(t, value_dim), qkvg, hn_w[None, :],
                   out_w.astype(MXU_DTYPE), h, g_col=2 * key_dim + value_dim)


def kernel(x, p, norm_mix, norm_ffn, norm_ple, ple_gate, ple_proj, ffn_w1, ffn_w3, ffn_w2, final_norm,
           ssd_in_w, ssd_conv_w, ssd_conv_b, ssd_dt_bias, ssd_a_log, ssd_d, ssd_norm, ssd_out_w,
           hy_in_w, hy_in_b, hy_conv_w, hy_conv_b, hy_f_w1, hy_f_b1, hy_f_w2, hy_f_b2, hy_f_w3,
           hy_sin_freq, hy_skip, hy_out_w, hy_out_b,
           gla_in_w, gla_gk_w, gla_gk_b, gla_norm, gla_out_w):
    bsz, seq, d = x.shape
    depth = p.shape[0]
    t = bsz * seq
    h = x.reshape(t, d)
    for i in range(depth):
        kind, j = i % N_MIXERS, i // N_MIXERS
        nw = norm_mix[i][None, :]
        if kind == 0:
            h = mamba2_mixer(h, bsz, nw, ssd_in_w[j], ssd_conv_w[j], ssd_conv_b[j], ssd_dt_bias[j], ssd_a_log[j],
                             ssd_d[j], ssd_norm[j], ssd_out_w[j])
        elif kind == 1:
            h = hyena_mixer(h, bsz, nw, hy_in_w[j], hy_in_b[j], hy_conv_w[j], hy_conv_b[j], hy_f_w1[j], hy_f_b1[j],
                            hy_f_w2[j], hy_f_b2[j], hy_f_w3[j], hy_sin_freq[j], hy_skip[j], hy_out_w[j], hy_out_b[j])
        else:
            h = gla_mixer(h, bsz, nw, gla_in_w[j], gla_gk_w[j], gla_gk_b[j], gla_norm[j], gla_out_w[j])
        h = ffn_ple(h, norm_ffn[i][None, :], ffn_w1[i].astype(MXU_DTYPE), ffn_w3[i].astype(MXU_DTYPE),
                    ffn_w2[i].astype(MXU_DTYPE), p[i].reshape(t, -1), norm_ple[i][None, :],
                    ple_gate[i].astype(MXU_DTYPE), ple_proj[i].astype(MXU_DTYPE), final_norm[None, :],
                    final=(i == depth - 1))
    return h.reshape(bsz, seq, d)
```

```python
import functools
import math

import numpy as np
import jax
import jax.numpy as jnp
from jax import lax
from jax.experimental import pallas as pl
from jax.experimental.pallas import tpu as pltpu

F32 = jnp.float32
MXU_DTYPE = jnp.bfloat16
EXACT = lax.Precision.HIGHEST

NORM_EPS = 1e-6
N_MIXERS = 3

VMEM_LIMIT_BYTES = 56 * 1024 * 1024
ROW_TILE = 512
PROJ_ROW_TILE = 1024
LANES = 128
FFN_SLICE = 512

SSD_HEADDIM = 64
SSD_GROUPS = 8
SSD_STATE = 128
SSD_CHUNK = 128
SSD_PAIR = 2 * SSD_HEADDIM
assert SSD_PAIR == LANES == SSD_CHUNK
GLA_HEADS = 4
GLA_CHUNK = 64
GLA_BLOCK = 256
GLA_SUB = 16
GLA_GATE_RANK = 16
GLA_GATE_NORMALIZER = 16.0
HY_PART_TILE = 256
HY_BANDS = 16
HY_DECAY_TARGET = 1e-2
HY_FAST_DECAY_PCT = 0.3
HY_SLOW_DECAY_PCT = 1.5
FFT_N1 = 64


def _cparams(sem):
    return pltpu.CompilerParams(dimension_semantics=sem, vmem_limit_bytes=VMEM_LIMIT_BYTES)


def _dot(a, b):
    return jnp.dot(a.astype(MXU_DTYPE), b.astype(MXU_DTYPE), preferred_element_type=F32)


def _dot_nt(a, b):
    return lax.dot_general(a.astype(MXU_DTYPE), b.astype(MXU_DTYPE), (((1,), (1,)), ((), ())),
                           preferred_element_type=F32)


def _dot_tn(a, b):
    return lax.dot_general(a.astype(MXU_DTYPE), b.astype(MXU_DTYPE), (((0,), (0,)), ((), ())),
                           preferred_element_type=F32)


def _dot_exact(a, b):
    return jnp.dot(a, b, preferred_element_type=F32, precision=EXACT)


def _split3(x):
    hi = x.astype(MXU_DTYPE)
    rest = x - hi.astype(F32)
    mid = rest.astype(MXU_DTYPE)
    lo = (rest - mid.astype(F32)).astype(MXU_DTYPE)
    return hi, mid, lo


def _mask_times(mask, x):
    m = jnp.where(mask, 1.0, 0.0).astype(MXU_DTYPE)
    return jnp.dot(jnp.concatenate([m, m, m], axis=1), jnp.concatenate(_split3(x), axis=0),
                   preferred_element_type=F32)


def _times_mask(x, mask):
    m = jnp.where(mask, 1.0, 0.0).astype(MXU_DTYPE)
    return jnp.dot(jnp.concatenate(_split3(x), axis=1), jnp.concatenate([m, m, m], axis=0),
                   preferred_element_type=F32)


def _rms(x, w):
    return x * lax.rsqrt(jnp.mean(x * x, axis=-1, keepdims=True) + NORM_EPS) * w


def _sigmoid(x):
    return 1.0 / (1.0 + jnp.exp(-x))


def _silu(x):
    return x * _sigmoid(x)


def _softplus(x):
    return jnp.maximum(x, 0.0) + jnp.log(1.0 + jnp.exp(-jnp.abs(x)))


def _norm_linear_kernel(x_ref, nw_ref, w_ref, b_ref, o_ref, xn_ref):
    @pl.when(pl.program_id(1) == 0)
    def _():
        xn_ref[...] = _rms(x_ref[...], nw_ref[...]).astype(xn_ref.dtype)

    o_ref[...] = jnp.dot(xn_ref[...], w_ref[...], preferred_element_type=F32) + b_ref[...]


def norm_linear(x, nw, w, b, col_tile):
    t, d = x.shape
    n = w.shape[1]
    tm = min(PROJ_ROW_TILE, t)
    return pl.pallas_call(
        _norm_linear_kernel,
        grid=(t // tm, n // col_tile),
        in_specs=[pl.BlockSpec((tm, d), lambda i, j: (i, 0)),
                  pl.BlockSpec((1, d), lambda i, j: (0, 0)),
                  pl.BlockSpec((d, col_tile), lambda i, j: (0, j)),
                  pl.BlockSpec((1, col_tile), lambda i, j: (0, j))],
        out_specs=pl.BlockSpec((tm, col_tile), lambda i, j: (i, j)),
        out_shape=jax.ShapeDtypeStruct((t, n), F32),
        scratch_shapes=[pltpu.VMEM((tm, d), MXU_DTYPE)],
        compiler_params=_cparams(("parallel", "arbitrary")),
        name="norm_linear",
    )(x, nw, w, b)


def _ffn_ple_kernel(h_ref, nf_ref, w1_ref, w3_ref, w2_ref, p_ref, npl_ref, g_ref, pp_ref, fn_ref,
                    o_ref, xn_ref, acc_ref, *, final):
    f = pl.program_id(1)

    @pl.when(f == 0)
    def _():
        xn_ref[...] = _rms(h_ref[...], nf_ref[...]).astype(xn_ref.dtype)
        acc_ref[...] = jnp.zeros_like(acc_ref)

    xn = xn_ref[...]
    tf = w1_ref.shape[1]
    total = None
    for c0 in range(0, tf, FFN_SLICE):
        cs = slice(c0, min(c0 + FFN_SLICE, tf))
        a = jnp.dot(xn, w1_ref[:, cs], preferred_element_type=F32)
        b = jnp.dot(xn, w3_ref[:, cs], preferred_element_type=F32)
        part = _dot(_silu(a) * b, w2_ref[cs, :])
        total = part if total is None else total + part
    acc_ref[...] += total

    @pl.when(f == pl.num_programs(1) - 1)
    def _():
        h2 = h_ref[...] + acc_ref[...]
        gate = _sigmoid(_dot(_rms(h2, npl_ref[...]), g_ref[...]))
        h3 = h2 + gate * _dot(p_ref[...], pp_ref[...])
        if final:
            h3 = _rms(h3, fn_ref[...])
        o_ref[...] = h3


def ffn_ple(h, nf, w1, w3, w2, p, npl, g, pp, fn, final):
    t, d = h.shape
    dff = w1.shape[1]
    tf = dff // 2
    pdim = p.shape[1]
    tm = min(ROW_TILE, t)
    row = lambda i, f: (i, 0)
    const = lambda i, f: (0, 0)
    return pl.pallas_call(
        functools.partial(_ffn_ple_kernel, final=final),
        grid=(t // tm, dff // tf),
        in_specs=[pl.BlockSpec((tm, d), row),
                  pl.BlockSpec((1, d), const),
                  pl.BlockSpec((d, tf), lambda i, f: (0, f)),
                  pl.BlockSpec((d, tf), lambda i, f: (0, f)),
                  pl.BlockSpec((tf, d), lambda i, f: (f, 0)),
                  pl.BlockSpec((tm, pdim), row),
                  pl.BlockSpec((1, d), const),
                  pl.BlockSpec((d, d), const),
                  pl.BlockSpec((pdim, d), const),
                  pl.BlockSpec((1, d), const)],
        out_specs=pl.BlockSpec((tm, d), row),
        out_shape=jax.ShapeDtypeStruct((t, d), F32),
        scratch_shapes=[pltpu.VMEM((tm, d), MXU_DTYPE), pltpu.VMEM((tm, d), F32)],
        compiler_params=_cparams(("parallel", "arbitrary")),
        name="ffn_ple",
    )(h, nf, w1, w3, w2, p, npl, g, pp, fn)


CONV_HALO = 16
CONV_ROWS = 256


def _norm_linear_conv_kernel(xp_ref, x_ref, xn_ref, nw_ref, w_ref, b_ref, cw_ref, cb_ref, *rest,
                             taps, tiles_per_seq, epilogue):
    *out_refs, xs_ref = rest
    tm = x_ref.shape[0]
    i = pl.program_id(0)

    @pl.when(pl.program_id(1) == 0)
    def _():
        nw = nw_ref[...]
        xs_ref[0:CONV_HALO, :] = _rms(xp_ref[...], nw).astype(xs_ref.dtype)
        xs_ref[CONV_HALO:CONV_HALO + tm, :] = _rms(x_ref[...], nw).astype(xs_ref.dtype)
        xs_ref[CONV_HALO + tm:, :] = _rms(xn_ref[...], nw).astype(xs_ref.dtype)

    first = (i % tiles_per_seq) == 0
    last = (i % tiles_per_seq) == tiles_per_seq - 1
    rows = min(CONV_ROWS, tm)
    nchunks = tm // rows
    for r in range(nchunks):
        base = r * rows
        win = jnp.dot(xs_ref[base:base + rows + 2 * CONV_HALO, :], w_ref[...],
                      preferred_element_type=F32) + b_ref[...]
        top, bottom = win[:CONV_HALO], win[CONV_HALO + rows:]
        if r == 0:
            top = jnp.where(first, 0.0, top)
        if r == nchunks - 1:
            bottom = jnp.where(last, 0.0, bottom)
        win = jnp.concatenate([top, win[CONV_HALO:CONV_HALO + rows], bottom], axis=0)
        acc = cb_ref[...]
        for k in range(taps):
            lo = CONV_HALO + k - taps // 2
            acc = acc + cw_ref[k:k + 1, :] * win[lo:lo + rows, :]
        epilogue(out_refs, slice(base, base + rows), acc)


def _silu_epilogue(out_refs, rows, acc):
    out_refs[0][rows, :] = _silu(acc)


def _hyena_epilogue(out_refs, rows, acc):
    c = acc.shape[1] // 3
    out_refs[0][rows, :] = acc[:, :c]
    out_refs[1][rows, :] = acc[:, 2 * c:] * acc[:, c:2 * c]


def norm_linear_conv(x, seq, nw, w, b, conv_w, conv_b, col_tile, epilogue, out_widths):
    t, d = x.shape
    n = w.shape[1]
    taps = conv_w.shape[0]
    tm = min(PROJ_ROW_TILE, seq)
    per_tile = tm // CONV_HALO
    last_halo = t // CONV_HALO - 1
    return pl.pallas_call(
        functools.partial(_norm_linear_conv_kernel, taps=taps, tiles_per_seq=seq // tm, epilogue=epilogue),
        grid=(t // tm, n // col_tile),
        in_specs=[pl.BlockSpec((CONV_HALO, d), lambda i, j: (jnp.maximum(i * per_tile - 1, 0), 0)),
                  pl.BlockSpec((tm, d), lambda i, j: (i, 0)),
                  pl.BlockSpec((CONV_HALO, d), lambda i, j: (jnp.minimum((i + 1) * per_tile, last_halo), 0)),
                  pl.BlockSpec((1, d), lambda i, j: (0, 0)),
                  pl.BlockSpec((d, col_tile), lambda i, j: (0, j)),
                  pl.BlockSpec((1, col_tile), lambda i, j: (0, j)),
                  pl.BlockSpec((taps, col_tile), lambda i, j: (0, j)),
                  pl.BlockSpec((1, col_tile), lambda i, j: (0, j))],
        out_specs=[pl.BlockSpec((tm, ow), lambda i, j: (i, j)) for ow in out_widths],
        out_shape=[jax.ShapeDtypeStruct((t, n // col_tile * ow), F32) for ow in out_widths],
        scratch_shapes=[pltpu.VMEM((tm + 2 * CONV_HALO, d), MXU_DTYPE)],
        compiler_params=_cparams(("parallel", "arbitrary")),
        name="norm_linear_conv",
    )(x, x, x, nw, w, b, conv_w, conv_b)


def _tri(n, reverse):
    row = lax.broadcasted_iota(jnp.int32, (n, n), 0)
    col = lax.broadcasted_iota(jnp.int32, (n, n), 1)
    return (row <= col) if reverse else (row >= col)


def _ssd_direction(xs_ref, b_ref, c_ref, dt_ref, dtb_ref, alog_ref, st_ref, y_ref, *, lane0, reverse):
    t = SSD_CHUNK
    heads = xs_ref.shape[1] // SSD_HEADDIM
    per_group = heads // SSD_GROUPS
    gw = per_group * SSD_HEADDIM
    tri = _tri(t, reverse)

    dt_all = _softplus(dt_ref[...] + dtb_ref[...])
    da_all = dt_all * (-jnp.exp(alog_ref[...]))
    da = da_all[:, lane0:lane0 + heads]
    dt_t = dt_all.T[lane0:lane0 + heads, :]
    da_t = da_all.T[lane0:lane0 + heads, :]
    acum = _mask_times(tri, da)
    acum_t = _times_mask(da_t, _tri(t, not reverse))
    tot_c = acum_t[:, 0:1] if reverse else acum_t[:, t - 1:t]
    w_t = jnp.exp(tot_c - acum_t) * dt_t
    etot = jnp.exp(jnp.sum(da, axis=0, keepdims=True))

    xs = xs_ref[...]
    first = lax.broadcasted_iota(jnp.int32, (t, SSD_PAIR), 1) < SSD_HEADDIM
    for g in range(SSD_GROUPS):
        bg_t = b_ref[:, g * SSD_STATE:(g + 1) * SSD_STATE].T
        cg = c_ref[:, g * SSD_STATE:(g + 1) * SSD_STATE].astype(MXU_DTYPE)
        cb = _dot(cg, bg_t)
        st = st_ref[g]
        gl = slice(g * gw, (g + 1) * gw)
        y_intra, e_acum, st_new = [], [], []
        for pr in range(per_group // 2):
            h0 = g * per_group + 2 * pr
            x_pair = xs[:, h0 * SSD_HEADDIM:(h0 + 2) * SSD_HEADDIM]
            rhs = jnp.concatenate([jnp.where(first, x_pair, 0.0),
                                   jnp.where(first, 0.0, x_pair)], axis=0).astype(MXU_DTYPE)
            cols = [jnp.broadcast_to(acum[:, h:h + 1], (t, SSD_PAIR)) for h in (h0, h0 + 1)]
            scores = [cb * jnp.exp(jnp.where(tri, cols[k] - acum_t[h0 + k:h0 + k + 1, :], -jnp.inf))
                      * dt_t[h0 + k:h0 + k + 1, :] for k in range(2)]
            y_intra.append(_dot(jnp.concatenate(scores, axis=1), rhs))
            e_acum.append(jnp.where(first, jnp.exp(cols[0]), jnp.exp(cols[1])))
            e_tot = jnp.where(first[0:1], jnp.broadcast_to(etot[:, h0:h0 + 1], (1, SSD_PAIR)),
                              jnp.broadcast_to(etot[:, h0 + 1:h0 + 2], (1, SSD_PAIR)))
            b_weighted = jnp.concatenate([bg_t * w_t[h0:h0 + 1, :], bg_t * w_t[h0 + 1:h0 + 2, :]], axis=1)
            st_new.append(st[:, pr * SSD_PAIR:(pr + 1) * SSD_PAIR] * e_tot + _dot(b_weighted, rhs))
        y_ref[:, gl] = _dot(cg, st) * jnp.concatenate(e_acum, axis=1) + jnp.concatenate(y_intra, axis=1)
        st_ref[g] = jnp.concatenate(st_new, axis=1)


def _ssd_scan_kernel(xf_ref, bf_ref, cf_ref, dtf_ref, xb_ref, bb_ref, cb_ref, dtb_ref,
                     bias_ref, alog_ref, yf_ref, yb_ref, stf_ref, stb_ref):
    @pl.when(pl.program_id(1) == 0)
    def _():
        stf_ref[...] = jnp.zeros_like(stf_ref)
        stb_ref[...] = jnp.zeros_like(stb_ref)

    heads = xf_ref.shape[1] // SSD_HEADDIM
    _ssd_direction(xf_ref, bf_ref, cf_ref, dtf_ref, bias_ref, alog_ref, stf_ref, yf_ref, lane0=0, reverse=False)
    _ssd_direction(xb_ref, bb_ref, cb_ref, dtb_ref, bias_ref, alog_ref, stb_ref, yb_ref, lane0=heads, reverse=True)


def ssd_scan(xbc, dt_raw, dt_bias, a_log):
    bsz, seq, width = xbc.shape
    gn = SSD_GROUPS * SSD_STATE
    d_inner = width - 2 * gn
    t = SSD_CHUNK
    nc = seq // t
    xblk = d_inner // gn
    fwd = lambda b, c: (b, c)
    bwd = lambda b, c: (b, nc - 1 - c)

    def specs(pos):
        return [pl.BlockSpec((None, t, d_inner), lambda b, c: (*pos(b, c), 0)),
                pl.BlockSpec((None, t, gn), lambda b, c: (*pos(b, c), xblk)),
                pl.BlockSpec((None, t, gn), lambda b, c: (*pos(b, c), xblk + 1)),
                pl.BlockSpec((None, t, LANES), lambda b, c: (*pos(b, c), 0))]

    const = pl.BlockSpec((1, LANES), lambda b, c: (0, 0))
    state = pltpu.VMEM((SSD_GROUPS, SSD_STATE, d_inner // SSD_GROUPS), F32)
    return pl.pallas_call(
        _ssd_scan_kernel,
        grid=(bsz, nc),
        in_specs=specs(fwd) + specs(bwd) + [const, const],
        out_specs=[pl.BlockSpec((None, t, d_inner), lambda b, c: (*fwd(b, c), 0)),
                   pl.BlockSpec((None, t, d_inner), lambda b, c: (*bwd(b, c), 0))],
        out_shape=[jax.ShapeDtypeStruct((bsz, seq, d_inner), F32)] * 2,
        scratch_shapes=[state, state],
        compiler_params=_cparams(("parallel", "arbitrary")),
        name="ssd_scan",
    )(xbc, xbc, xbc, dt_raw, xbc, xbc, xbc, dt_raw, dt_bias, a_log)


def _ssd_out_kernel(yf_ref, yb_ref, xs_ref, z_ref, dsk_ref, nw_ref, w_ref, res_ref, o_ref):
    y = (yf_ref[...] + yb_ref[...] + xs_ref[...] * dsk_ref[...]) * _silu(z_ref[...])
    gw = y.shape[1] // SSD_GROUPS
    y = jnp.concatenate([_rms(y[:, g * gw:(g + 1) * gw], nw_ref[:, g * gw:(g + 1) * gw])
                         for g in range(SSD_GROUPS)], axis=1)
    o_ref[...] = res_ref[...] + _dot(y, w_ref[...])


def ssd_out(y_f, y_b, xbc, zx, d_skip, norm_w, out_w, res):
    t, d_inner = y_f.shape
    d = out_w.shape[1]
    tm = min(ROW_TILE, t)
    row = lambda i: (i, 0)
    const = lambda i: (0, 0)
    return pl.pallas_call(
        _ssd_out_kernel,
        grid=(t // tm,),
        in_specs=[pl.BlockSpec((tm, d_inner), row), pl.BlockSpec((tm, d_inner), row),
                  pl.BlockSpec((tm, d_inner), row), pl.BlockSpec((tm, d_inner), row),
                  pl.BlockSpec((1, d_inner), const), pl.BlockSpec((1, d_inner), const),
                  pl.BlockSpec((d_inner, d), const), pl.BlockSpec((tm, d), row)],
        out_specs=pl.BlockSpec((tm, d), row),
        out_shape=jax.ShapeDtypeStruct((t, d), F32),
        compiler_params=_cparams(("parallel",)),
        name="ssd_out",
    )(y_f, y_b, xbc, zx, d_skip, norm_w, out_w, res)


def _pad_rows(x, r0, total):
    parts = []
    if r0 > 0:
        parts.append(jnp.zeros((r0, x.shape[1]), x.dtype))
    parts.append(x)
    if r0 + x.shape[0] < total:
        parts.append(jnp.zeros((total - r0 - x.shape[0], x.shape[1]), x.dtype))
    return jnp.concatenate(parts, axis=0) if len(parts) > 1 else x


def _gla_chunk(q_ref, k_ref, v_ref, o_ref, g, states, rs, *, reverse):
    ch = GLA_CHUNK
    kd = q_ref.shape[1]
    dk = kd // GLA_HEADS
    dv = v_ref.shape[1] // GLA_HEADS
    nsub = ch // GLA_SUB
    tri = _tri(ch, reverse)
    eye = lax.broadcasted_iota(jnp.int32, (dk, dk), 0) == lax.broadcasted_iota(jnp.int32, (dk, dk), 1)
    bc = _mask_times(tri, g)
    tot = jnp.sum(g, axis=0, keepdims=True)
    q = q_ref[rs, :] * (dk ** -0.5)
    k = k_ref[rs, :]
    v = v_ref[rs, :].astype(MXU_DTYPE)
    q_in = q * jnp.exp(bc)
    k_st = k * jnp.exp(tot - bc)
    e_tot = jnp.exp(tot)
    new_states = []
    for h in range(GLA_HEADS):
        ks = slice(h * dk, (h + 1) * dk)
        vs = slice(h * dv, (h + 1) * dv)
        bch, qh, kh = bc[:, ks], q[:, ks], k[:, ks]
        q_segs, k_segs = [], []
        for i in range(nsub):
            l0, l1 = i * GLA_SUB, (i + 1) * GLA_SUB
            if reverse:
                s0, s1 = l0, ch
                ref = bch[l1:l1 + 1] if l1 < ch else 0.0
            else:
                s0, s1 = 0, l1
                ref = bch[l0 - 1:l0] if l0 > 0 else 0.0
            q_segs.append(_pad_rows(qh[l0:l1] * jnp.exp(bch[l0:l1] - ref), l0, ch))
            k_segs.append(_pad_rows(kh[s0:s1] * jnp.exp(ref - bch[s0:s1]), s0, ch))
        att = _dot_nt(jnp.concatenate(q_segs, axis=1), jnp.concatenate(k_segs, axis=1))
        att = jnp.where(tri, att, 0.0)
        st = states[h]
        o_ref[rs, vs] = _dot(jnp.concatenate([q_in[:, ks], att], axis=1),
                             jnp.concatenate([st.astype(MXU_DTYPE), v[:, vs]], axis=0))
        e_col = jnp.sum(jnp.where(eye, jnp.broadcast_to(e_tot[:, ks], (dk, dk)), 0.0), axis=1, keepdims=True)
        new_states.append(st * e_col + _dot_tn(k_st[:, ks], v[:, vs]))
    return new_states


def _gla_gate(gl_ref, wgk_ref, bgk_ref, cols):
    x = _dot(gl_ref[...], wgk_ref[:, cols]) + bgk_ref[:, cols]
    return -_softplus(-x) * (1.0 / GLA_GATE_NORMALIZER)


def _gla_scan_kernel(qf_ref, kf_ref, vf_ref, glf_ref, qb_ref, kb_ref, vb_ref, glb_ref, wgk_ref, bgk_ref,
                     of_ref, ob_ref, stf_ref, stb_ref):
    @pl.when(pl.program_id(1) == 0)
    def _():
        stf_ref[...] = jnp.zeros_like(stf_ref)
        stb_ref[...] = jnp.zeros_like(stb_ref)

    ch = GLA_CHUNK
    rows, kd = qf_ref.shape
    nchunks = rows // ch
    g_f = _gla_gate(glf_ref, wgk_ref, bgk_ref, slice(0, kd))
    g_b = _gla_gate(glb_ref, wgk_ref, bgk_ref, slice(kd, 2 * kd))
    st_f = [stf_ref[h] for h in range(GLA_HEADS)]
    st_b = [stb_ref[h] for h in range(GLA_HEADS)]
    for i in range(nchunks):
        rf = slice(i * ch, (i + 1) * ch)
        rb = slice((nchunks - 1 - i) * ch, (nchunks - i) * ch)
        st_f = _gla_chunk(qf_ref, kf_ref, vf_ref, of_ref, g_f[rf], st_f, rf, reverse=False)
        st_b = _gla_chunk(qb_ref, kb_ref, vb_ref, ob_ref, g_b[rb], st_b, rb, reverse=True)
    for h in range(GLA_HEADS):
        stf_ref[h] = st_f[h]
        stb_ref[h] = st_b[h]


def gla_scan(qkvg, gl, wgk, bgk, key_dim, value_dim):
    bsz, seq, _ = qkvg.shape
    ch = min(GLA_BLOCK, seq)
    nc = seq // ch
    fwd = lambda b, c: (b, c)
    bwd = lambda b, c: (b, nc - 1 - c)

    def specs(pos):
        return [pl.BlockSpec((None, ch, key_dim), lambda b, c: (*pos(b, c), 0)),
                pl.BlockSpec((None, ch, key_dim), lambda b, c: (*pos(b, c), 1)),
                pl.BlockSpec((None, ch, value_dim), lambda b, c: (*pos(b, c), (2 * key_dim) // value_dim)),
                pl.BlockSpec((None, ch, LANES), lambda b, c: (*pos(b, c), 0))]

    state = pltpu.VMEM((GLA_HEADS, key_dim // GLA_HEADS, value_dim // GLA_HEADS), F32)
    return pl.pallas_call(
        _gla_scan_kernel,
        grid=(bsz, nc),
        in_specs=specs(fwd) + specs(bwd) + [pl.BlockSpec((LANES, 2 * key_dim), lambda b, c: (0, 0)),
                                            pl.BlockSpec((1, 2 * key_dim), lambda b, c: (0, 0))],
        out_specs=[pl.BlockSpec((None, ch, value_dim), lambda b, c: (*fwd(b, c), 0)),
                   pl.BlockSpec((None, ch, value_dim), lambda b, c: (*bwd(b, c), 0))],
        out_shape=[jax.ShapeDtypeStruct((bsz, seq, value_dim), F32)] * 2,
        scratch_shapes=[state, state],
        compiler_params=_cparams(("parallel", "arbitrary")),
        name="gla_scan",
    )(qkvg, qkvg, qkvg, gl, qkvg, qkvg, qkvg, gl, wgk, bgk)


def _gla_out_kernel(of_ref, ob_ref, g_ref, nw_ref, w_ref, res_ref, o_ref):
    o = of_ref[...] + ob_ref[...]
    dv = o.shape[1] // GLA_HEADS
    o = jnp.concatenate([_rms(o[:, h * dv:(h + 1) * dv], nw_ref[...]) for h in range(GLA_HEADS)], axis=1)
    o_ref[...] = res_ref[...] + _dot(o * _silu(g_ref[...]), w_ref[...])


def gla_out(o_f, o_b, qkvg, norm_w, out_w, res, g_col):
    t, vd = o_f.shape
    d = out_w.shape[1]
    tm = min(ROW_TILE, t)
    row = lambda i: (i, 0)
    const = lambda i: (0, 0)
    return pl.pallas_call(
        _gla_out_kernel,
        grid=(t // tm,),
        in_specs=[pl.BlockSpec((tm, vd), row), pl.BlockSpec((tm, vd), row),
                  pl.BlockSpec((tm, vd), lambda i: (i, g_col // vd)),
                  pl.BlockSpec((1, vd // GLA_HEADS), const),
                  pl.BlockSpec((vd, d), const), pl.BlockSpec((tm, d), row)],
        out_specs=pl.BlockSpec((tm, d), row),
        out_shape=jax.ShapeDtypeStruct((t, d), F32),
        compiler_params=_cparams(("parallel",)),
        name="gla_out",
    )(o_f, o_b, qkvg, norm_w, out_w, res)


def _hy_filter_kernel(w1_ref, b1_ref, w2_ref, b2_ref, w3_ref, freq_ref, delta_ref, o_ref, *, seq):
    tr = o_ref.shape[0]
    r = pl.program_id(1) * tr + lax.broadcasted_iota(jnp.int32, (tr, 1), 0)
    pos = jnp.where(r < seq, r, 2 * seq - r).astype(F32)
    t = pos / (seq - 1.0)
    w = (2.0 * math.pi) * pos / seq
    band = lax.broadcasted_iota(jnp.int32, (1, HY_BANDS), 1).astype(F32)
    bands = 1e-4 + band * ((HY_BANDS - 1 - 1e-4) / (HY_BANDS - 1))
    zw1 = (t * w1_ref[0:1, :] + _dot_exact(jnp.cos(bands * w), w1_ref[1:1 + HY_BANDS, :])
           + _dot_exact(-jnp.sin(bands * w), w1_ref[1 + HY_BANDS:1 + 2 * HY_BANDS, :]))
    freq = freq_ref[...]
    hid = jnp.sin(freq * (zw1 + b1_ref[...]))
    hid = jnp.sin(freq * (_dot_exact(hid, w2_ref[...]) + b2_ref[...]))
    filt = _dot_exact(hid, w3_ref[...]) * jnp.exp(-t * delta_ref[...])
    o_ref[...] = jnp.where(r == seq, 0.0, filt)


def hy_filter(seq, w1, b1, w2, b2, w3, freq, width, tr=512):
    n = 2 * seq
    tr = min(tr, seq)
    tc = 512
    emb, hidden = w1.shape
    min_decay = math.log(HY_DECAY_TARGET) / HY_SLOW_DECAY_PCT
    max_decay = math.log(HY_DECAY_TARGET) / HY_FAST_DECAY_PCT
    deltas = jnp.abs(jnp.linspace(min_decay, max_decay, width, dtype=F32))[None, :]
    nct = width // tc
    half = seq // tr
    const = lambda j, i: (0, 0)
    return pl.pallas_call(
        functools.partial(_hy_filter_kernel, seq=seq),
        grid=(nct, n // tr),
        in_specs=[pl.BlockSpec((emb, hidden), const), pl.BlockSpec((1, hidden), const),
                  pl.BlockSpec((hidden, hidden), const), pl.BlockSpec((1, hidden), const),
                  pl.BlockSpec((hidden, tc), lambda j, i: (0, j + nct * (i // half))),
                  pl.BlockSpec((1, hidden), const),
                  pl.BlockSpec((1, tc), lambda j, i: (0, j))],
        out_specs=pl.BlockSpec((tr, tc), lambda j, i: (i, j)),
        out_shape=jax.ShapeDtypeStruct((n, width), F32),
        compiler_params=_cparams(("parallel", "parallel")),
        name="hy_filter",
    )(w1, b1, w2, b2, w3, freq, deltas)


def _stack_complex(m):
    return np.concatenate([np.concatenate([m.real, -m.imag], axis=-1),
                           np.concatenate([m.imag, m.real], axis=-1)], axis=-2)


@functools.lru_cache(maxsize=None)
def _dft_tables(n, n1):
    n2 = n // n1
    h1 = n1 // 2
    ang = -2.0 * np.pi / n
    k1 = np.arange(n1)[None, :, None]
    nn1 = np.arange(n1)[None, None, :]
    nn2 = np.arange(n2)[:, None, None]
    g1_full = np.exp(1j * ang * k1 * (n2 * nn1 + nn2))
    g1 = _stack_complex(g1_full[:, :, :h1])
    g1_real = np.concatenate([g1_full.real, g1_full.imag], axis=-2)
    kk2 = np.arange(n2)[:, None]
    f2 = np.exp(-2j * np.pi * kk2 * np.arange(n2)[None, :] / n2)
    f2s = _stack_complex(f2)
    f3s = _stack_complex(np.conj(f2))
    out1 = np.arange(h1)[None, :, None]
    g4 = np.exp(-1j * ang * (n2 * out1 + nn2) * np.arange(n1)[None, None, :]) / n
    g4s = _stack_complex(g4)
    cast = lambda a: jnp.asarray(a, dtype=F32).astype(MXU_DTYPE)
    return cast(g1), cast(g1_real), cast(f2s), cast(f3s), cast(g4s)


FFT_UNROLL = 8
FFT_SLABS = 4
FFT_PAD = 8


def _load_complex_strided(ref, start, rows, stride):
    return jnp.concatenate([ref[0, pl.ds(start, rows, stride=stride), :],
                            ref[1, pl.ds(start, rows, stride=stride), :]], axis=0)


def _store_complex_slab(ref, base, rows, out):
    ref[0, pl.ds(base, rows), :] = out[:rows]
    ref[1, pl.ds(base, rows), :] = out[rows:]


def _shared_matrix_stage(load, store, mat_ref, count):
    def body(jj, carry):
        j0 = jj * FFT_SLABS
        x = jnp.concatenate([load(j0 + u) for u in range(FFT_SLABS)], axis=1).astype(MXU_DTYPE)
        out = jnp.dot(mat_ref[...], x, preferred_element_type=F32)
        for u in range(FFT_SLABS):
            store(j0 + u, out[:, u * LANES:(u + 1) * LANES])
        return carry

    lax.fori_loop(0, count // FFT_SLABS, body, 0, unroll=2)


def _per_slab_matrix_stage(load, store, mats_ref, count):
    def body(j, carry):
        store(j, jnp.dot(mats_ref[j], load(j).astype(MXU_DTYPE), preferred_element_type=F32))
        return carry

    lax.fori_loop(0, count, body, 0, unroll=FFT_UNROLL)


def _slab_base(j, rows):
    return pl.multiple_of(j * (rows + FFT_PAD), 8)


def _scratch_rows(n, n1):
    n2 = n // n1
    return max(n2 * (n1 + FFT_PAD), n1 * (n2 + FFT_PAD))


def _hy_spectrum_kernel(filt_ref, g1r_ref, f2s_ref, hf_ref, a_ref, *, n1):
    n = filt_ref.shape[0]
    n2 = n // n1
    _per_slab_matrix_stage(
        lambda j: filt_ref[pl.ds(j, n1, stride=n2), :],
        lambda j, out: _store_complex_slab(a_ref, _slab_base(j, n1), n1, out),
        g1r_ref, n2)
    _shared_matrix_stage(
        lambda j: _load_complex_strided(a_ref, j, n2, n1 + FFT_PAD),
        lambda j, out: _store_complex_slab(hf_ref, pl.multiple_of(j * n2, n2), n2, out),
        f2s_ref, n1)


def hy_spectrum(filt):
    n, c = filt.shape
    n1 = FFT_N1
    _, g1r, f2s, _, _ = _dft_tables(n, n1)
    return pl.pallas_call(
        functools.partial(_hy_spectrum_kernel, n1=n1),
        grid=(c // LANES,),
        in_specs=[pl.BlockSpec((n, LANES), lambda j: (0, j)),
                  pl.BlockSpec(g1r.shape, lambda j: (0, 0, 0)),
                  pl.BlockSpec(f2s.shape, lambda j: (0, 0))],
        out_specs=pl.BlockSpec((2, n, LANES), lambda j: (0, 0, j)),
        out_shape=jax.ShapeDtypeStruct((2, n, c), F32),
        scratch_shapes=[pltpu.VMEM((2, _scratch_rows(n, n1), LANES), F32)],
        compiler_params=_cparams(("parallel",)),
        name="hy_spectrum",
    )(filt, g1r, f2s)


def _hy_fftconv_kernel(v_ref, hf_ref, g1_ref, f2s_ref, f3s_ref, g4s_ref, o_ref, a_ref, y_ref, *, n1):
    seq = v_ref.shape[1]
    n = 2 * seq
    n2 = n // n1
    h1 = n1 // 2

    def multiply_by_spectrum(j, out):
        base = pl.multiple_of(j * n2, n2)
        xr, xi = out[:n2], out[n2:]
        hr = hf_ref[0, pl.ds(base, n2), :]
        hi = hf_ref[1, pl.ds(base, n2), :]
        y_ref[0, pl.ds(base, n2), :] = xr * hr - xi * hi
        y_ref[1, pl.ds(base, n2), :] = xr * hi + xi * hr

    def store_outputs(j, out):
        o_ref[0, pl.ds(j, h1, stride=n2), :] = out[:h1]
        o_ref[1, pl.ds(j, h1, stride=n2), :] = out[h1:]

    _per_slab_matrix_stage(
        lambda j: _load_complex_strided(v_ref, j, h1, n2),
        lambda j, out: _store_complex_slab(a_ref, _slab_base(j, n1), n1, out),
        g1_ref, n2)
    _shared_matrix_stage(lambda j: _load_complex_strided(a_ref, j, n2, n1 + FFT_PAD), multiply_by_spectrum,
                         f2s_ref, n1)
    _shared_matrix_stage(
        lambda j: jnp.concatenate([y_ref[0, pl.ds(pl.multiple_of(j * n2, n2), n2), :],
                                   y_ref[1, pl.ds(pl.multiple_of(j * n2, n2), n2), :]], axis=0),
        lambda j, out: _store_complex_slab(a_ref, _slab_base(j, n2), n2, out),
        f3s_ref, n1)
    _per_slab_matrix_stage(lambda j: _load_complex_strided(a_ref, j, n1, n2 + FFT_PAD), store_outputs, g4s_ref, n2)


def hy_fftconv(vx, hf):
    bsz, seq, c = vx.shape
    n = 2 * seq
    n1 = FFT_N1
    g1, _, f2s, f3s, g4s = _dft_tables(n, n1)
    once = pl.Buffered(1)
    tab3 = lambda a: pl.BlockSpec(a.shape, lambda j, p: (0, 0, 0), pipeline_mode=once)
    tab2 = lambda a: pl.BlockSpec(a.shape, lambda j, p: (0, 0), pipeline_mode=once)
    pair = pl.BlockSpec((None, 2, seq, LANES), lambda j, p: (p, 0, 0, j))
    out = pl.pallas_call(
        functools.partial(_hy_fftconv_kernel, n1=n1),
        grid=(c // LANES, bsz // 2),
        in_specs=[pair,
                  pl.BlockSpec((2, n, LANES), lambda j, p: (0, 0, j), pipeline_mode=once),
                  tab3(g1), tab2(f2s), tab2(f3s), tab3(g4s)],
        out_specs=pair,
        out_shape=jax.ShapeDtypeStruct((bsz // 2, 2, seq, c), F32),
        scratch_shapes=[pltpu.VMEM((2, _scratch_rows(n, n1), LANES), F32), pltpu.VMEM((2, n, LANES), F32)],
        compiler_params=_cparams(("parallel", "parallel")),
        name="hy_fftconv",
    )(vx.reshape(bsz // 2, 2, seq, c), hf, g1, f2s, f3s, g4s)
    return out.reshape(bsz, seq, c)


def _hy_out_kernel(conv_ref, vx_ref, x0_ref, skip_ref, w_ref, b_ref, res_ref, o_ref):
    y = (conv_ref[...] + vx_ref[...] * skip_ref[...]) * x0_ref[...]
    o_ref[...] = res_ref[...] + _dot(y, w_ref[...]) + b_ref[...]


def hy_out(conv, vx, x0c, skip, out_w, out_b, res):
    t, c = vx.shape
    d = out_w.shape[1]
    tm = min(ROW_TILE, t)
    row = lambda i: (i, 0)
    const = lambda i: (0, 0)
    return pl.pallas_call(
        _hy_out_kernel,
        grid=(t // tm,),
        in_specs=[pl.BlockSpec((tm, c), row), pl.BlockSpec((tm, c), row), pl.BlockSpec((tm, c), row),
                  pl.BlockSpec((1, c), const), pl.BlockSpec((c, d), const), pl.BlockSpec((1, d), const),
                  pl.BlockSpec((tm, d), row)],
        out_specs=pl.BlockSpec((tm, d), row),
        out_shape=jax.ShapeDtypeStruct((t, d), F32),
        compiler_params=_cparams(("parallel",)),
        name="hy_out",
    )(conv, vx, x0c, skip, out_w, out_b, res)


def _pad_cols(a, width):
    return jnp.pad(a, ((0, 0), (0, width - a.shape[1])))


def _col_tile(n):
    for tile in (1024, 768, 512, 256, 128):
        if n % tile == 0:
            return tile
    raise ValueError(f"unsupported projection width {n}")


def mamba2_mixer(h, bsz, norm_w, in_w, conv_w, conv_b, dt_bias, a_log, d_skip, gn_w, out_w):
    t, d = h.shape
    seq = t // bsz
    heads = a_log.shape[1]
    d_inner = heads * SSD_HEADDIM
    conv_dim = conv_w.shape[1]
    main = d_inner + conv_dim
    z = norm_linear(h, norm_w, in_w[:, :d_inner].astype(MXU_DTYPE), jnp.zeros((1, d_inner), F32), _col_tile(d_inner))
    (xbc,) = norm_linear_conv(h, seq, norm_w, in_w[:, d_inner:main].astype(MXU_DTYPE), jnp.zeros((1, conv_dim), F32),
                              conv_w, conv_b[None, :], _col_tile(conv_dim), _silu_epilogue, (_col_tile(conv_dim),))
    w_dt = _pad_cols(in_w[:, main:], LANES).astype(MXU_DTYPE)
    dt_raw = norm_linear(h, norm_w, w_dt, jnp.zeros((1, LANES), F32), LANES)
    y_f, y_b = ssd_scan(xbc.reshape(bsz, seq, conv_dim), dt_raw.reshape(bsz, seq, LANES),
                        _pad_cols(dt_bias.reshape(1, 2 * heads), LANES),
                        _pad_cols(a_log.reshape(1, 2 * heads), LANES))
    return ssd_out(y_f.reshape(t, d_inner), y_b.reshape(t, d_inner), xbc, z,
                   jnp.repeat(d_skip, SSD_HEADDIM)[None, :], gn_w[None, :], out_w.astype(MXU_DTYPE), h)


def hyena_mixer(h, bsz, norm_w, in_w, in_b, conv_w, conv_b, f_w1, f_b1, f_w2, f_b2, f_w3, sin_freq, skip,
                out_w, out_b):
    t, d = h.shape
    seq = t // bsz
    width = skip.shape[0]
    sub = HY_PART_TILE
    by_tile = lambda a: a.reshape(a.shape[0], 3, width // sub, sub).transpose(0, 2, 1, 3).reshape(a.shape[0], 3 * width)
    x0c, vx = norm_linear_conv(h, seq, norm_w, by_tile(in_w).astype(MXU_DTYPE), by_tile(in_b[None, :]),
                               by_tile(conv_w), by_tile(conv_b[None, :]), 3 * sub, _hyena_epilogue, (sub, sub))
    filt = hy_filter(seq, f_w1, f_b1[None, :], f_w2, f_b2[None, :], f_w3, sin_freq[None, :], width)
    conv = hy_fftconv(vx.reshape(bsz, seq, width), hy_spectrum(filt))
    return hy_out(conv.reshape(t, width), vx, x0c, skip[None, :], out_w.astype(MXU_DTYPE), out_b[None, :], h)


def gla_mixer(h, bsz, norm_w, in_w, gk_w, gk_b, hn_w, out_w):
    t, d = h.shape
    seq = t // bsz
    rank, key_dim = gk_w.shape[1], gk_w.shape[2]
    value_dim = out_w.shape[0]
    main = 2 * key_dim + 2 * value_dim
    qkvg = norm_linear(h, norm_w, in_w[:, :main].astype(MXU_DTYPE), jnp.zeros((1, main), F32), _col_tile(main))
    w_gl = _pad_cols(in_w[:, main:], LANES).astype(MXU_DTYPE)
    gl = norm_linear(h, norm_w, w_gl, jnp.zeros((1, LANES), F32), LANES)
    wgk = jnp.zeros((LANES, 2 * key_dim), F32)
    wgk = wgk.at[:rank, :key_dim].set(gk_w[0]).at[rank:2 * rank, key_dim:].set(gk_w[1]).astype(MXU_DTYPE)
    o_f, o_b = gla_scan(qkvg.reshape(bsz, seq, main), gl.reshape(bsz, seq, LANES), wgk,
                        gk_b.reshape(1, 2 * key_dim), key_dim, value_dim)
    return gla_out(o_f.reshape(t, value_dim), o_b.reshape(t, value_dim), qkvg, hn_w[None, :],
                   out_w.astype(MXU_DTYPE), h, g_col=2 * key_dim + value_dim)


def kernel(x, p, norm_mix, norm_ffn, norm_ple, ple_gate, ple_proj, ffn_w1, ffn_w3, ffn_w2, final_norm,
           ssd_in_w, ssd_conv_w, ssd_conv_b, ssd_dt_bias, ssd_a_log, ssd_d, ssd_norm, ssd_out_w,
           hy_in_w, hy_in_b, hy_conv_w, hy_conv_b, hy_f_w1, hy_f_b1, hy_f_w2, hy_f_b2, hy_f_w3,
           hy_sin_freq, hy_skip, hy_out_w, hy_out_b,
           gla_in_w, gla_gk_w, gla_gk_b, gla_norm, gla_out_w):
    bsz, seq, d = x.shape
    depth = p.shape[0]
    t = bsz * seq
    h = x.reshape(t, d)
    for i in range(depth):
        kind, j = i % N_MIXERS, i // N_MIXERS
        nw = norm_mix[i][None, :]
        if kind == 0:
            h = mamba2_mixer(h, bsz, nw, ssd_in_w[j], ssd_conv_w[j], ssd_conv_b[j], ssd_dt_bias[j], ssd_a_log[j],
                             ssd_d[j], ssd_norm[j], ssd_out_w[j])
        elif kind == 1:
            h = hyena_mixer(h, bsz, nw, hy_in_w[j], hy_in_b[j], hy_conv_w[j], hy_conv_b[j], hy_f_w1[j], hy_f_b1[j],
                            hy_f_w2[j], hy_f_b2[j], hy_f_w3[j], hy_sin_freq[j], hy_skip[j], hy_out_w[j], hy_out_b[j])
        else:
            h = gla_mixer(h, bsz, nw, gla_in_w[j], gla_gk_w[j], gla_gk_b[j], gla_norm[j], gla_out_w[j])
        h = ffn_ple(h, norm_ffn[i][None, :], ffn_w1[i].astype(MXU_DTYPE), ffn_w3[i].astype(MXU_DTYPE),
                    ffn_w2[i].astype(MXU_DTYPE), p[i].reshape(t, -1), norm_ple[i][None, :],
                    ple_gate[i].astype(MXU_DTYPE), ple_proj[i].astype(MXU_DTYPE), final_norm[None, :],
                    final=(i == depth - 1))
    return h.reshape(bsz, seq, d)
```

```python
import functools
import math

import numpy as np
import jax
import jax.numpy as jnp
from jax import lax
from jax.experimental import pallas as pl
from jax.experimental.pallas import tpu as pltpu

F32 = jnp.float32
MXU_DTYPE = jnp.bfloat16
EXACT = lax.Precision.HIGHEST

NORM_EPS = 1e-6
N_MIXERS = 3

VMEM_LIMIT_BYTES = 56 * 1024 * 1024
ROW_TILE = 512
PROJ_ROW_TILE = 1024
LANES = 128
FFN_SLICE = 512

SSD_HEADDIM = 64
SSD_GROUPS = 8
SSD_STATE = 128
SSD_CHUNK = 128
SSD_PAIR = 2 * SSD_HEADDIM
assert SSD_PAIR == LANES == SSD_CHUNK
GLA_HEADS = 4
GLA_CHUNK = 64
GLA_BLOCK = 256
GLA_SUB = 16
GLA_GATE_RANK = 16
GLA_GATE_NORMALIZER = 16.0
HY_PART_TILE = 256
HY_BANDS = 16
HY_DECAY_TARGET = 1e-2
HY_FAST_DECAY_PCT = 0.3
HY_SLOW_DECAY_PCT = 1.5
FFT_N1 = 64


def _cparams(sem):
    return pltpu.CompilerParams(dimension_semantics=sem, vmem_limit_bytes=VMEM_LIMIT_BYTES)


def _dot(a, b):
    return jnp.dot(a.astype(MXU_DTYPE), b.astype(MXU_DTYPE), preferred_element_type=F32)


def _dot_nt(a, b):
    return lax.dot_general(a.astype(MXU_DTYPE), b.astype(MXU_DTYPE), (((1,), (1,)), ((), ())),
                           preferred_element_type=F32)


def _dot_tn(a, b):
    return lax.dot_general(a.astype(MXU_DTYPE), b.astype(MXU_DTYPE), (((0,), (0,)), ((), ())),
                           preferred_element_type=F32)


def _dot_exact(a, b):
    return jnp.dot(a, b, preferred_element_type=F32, precision=EXACT)


def _split3(x):
    hi = x.astype(MXU_DTYPE)
    rest = x - hi.astype(F32)
    mid = rest.astype(MXU_DTYPE)
    lo = (rest - mid.astype(F32)).astype(MXU_DTYPE)
    return hi, mid, lo


def _mask_times(mask, x):
    m = jnp.where(mask, 1.0, 0.0).astype(MXU_DTYPE)
    return jnp.dot(jnp.concatenate([m, m, m], axis=1), jnp.concatenate(_split3(x), axis=0),
                   preferred_element_type=F32)


def _times_mask(x, mask):
    m = jnp.where(mask, 1.0, 0.0).astype(MXU_DTYPE)
    return jnp.dot(jnp.concatenate(_split3(x), axis=1), jnp.concatenate([m, m, m], axis=0),
                   preferred_element_type=F32)


def _rms(x, w):
    return x * lax.rsqrt(jnp.mean(x * x, axis=-1, keepdims=True) + NORM_EPS) * w


def _sigmoid(x):
    return 1.0 / (1.0 + jnp.exp(-x))


def _silu(x):
    return x * _sigmoid(x)


def _softplus(x):
    return jnp.maximum(x, 0.0) + jnp.log(1.0 + jnp.exp(-jnp.abs(x)))


def _norm_linear_kernel(x_ref, nw_ref, w_ref, ws_ref, o_ref, os_ref, xn_ref):
    @pl.when(pl.program_id(1) == 0)
    def _():
        xn_ref[...] = _rms(x_ref[...], nw_ref[...]).astype(xn_ref.dtype)
        os_ref[...] = jnp.dot(xn_ref[...], ws_ref[...], preferred_element_type=F32)

    o_ref[...] = jnp.dot(xn_ref[...], w_ref[...], preferred_element_type=F32)


def norm_linear(x, nw, w, w_side, col_tile):
    t, d = x.shape
    n = w.shape[1]
    tm = min(PROJ_ROW_TILE, t)
    return pl.pallas_call(
        _norm_linear_kernel,
        grid=(t // tm, n // col_tile),
        in_specs=[pl.BlockSpec((tm, d), lambda i, j: (i, 0)),
                  pl.BlockSpec((1, d), lambda i, j: (0, 0)),
                  pl.BlockSpec((d, col_tile), lambda i, j: (0, j)),
                  pl.BlockSpec((d, LANES), lambda i, j: (0, 0))],
        out_specs=[pl.BlockSpec((tm, col_tile), lambda i, j: (i, j)),
                   pl.BlockSpec((tm, LANES), lambda i, j: (i, 0))],
        out_shape=[jax.ShapeDtypeStruct((t, n), F32), jax.ShapeDtypeStruct((t, LANES), F32)],
        scratch_shapes=[pltpu.VMEM((tm, d), MXU_DTYPE)],
        compiler_params=_cparams(("parallel", "arbitrary")),
        name="norm_linear",
    )(x, nw, w, w_side)


def _ffn_ple_kernel(h_ref, nf_ref, w1_ref, w3_ref, w2_ref, p_ref, npl_ref, g_ref, pp_ref, fn_ref,
                    o_ref, xn_ref, acc_ref, *, final):
    f = pl.program_id(1)

    @pl.when(f == 0)
    def _():
        xn_ref[...] = _rms(h_ref[...], nf_ref[...]).astype(xn_ref.dtype)
        acc_ref[...] = jnp.zeros_like(acc_ref)

    xn = xn_ref[...]
    tf = w1_ref.shape[1]
    total = None
    for c0 in range(0, tf, FFN_SLICE):
        cs = slice(c0, min(c0 + FFN_SLICE, tf))
        a = jnp.dot(xn, w1_ref[:, cs], preferred_element_type=F32)
        b = jnp.dot(xn, w3_ref[:, cs], preferred_element_type=F32)
        part = _dot(_silu(a) * b, w2_ref[cs, :])
        total = part if total is None else total + part
    acc_ref[...] += total

    @pl.when(f == pl.num_programs(1) - 1)
    def _():
        h2 = h_ref[...] + acc_ref[...]
        gate = _sigmoid(_dot(_rms(h2, npl_ref[...]), g_ref[...]))
        h3 = h2 + gate * _dot(p_ref[...], pp_ref[...])
        if final:
            h3 = _rms(h3, fn_ref[...])
        o_ref[...] = h3


def ffn_ple(h, nf, w1, w3, w2, p, npl, g, pp, fn, final):
    t, d = h.shape
    dff = w1.shape[1]
    tf = dff // 2
    pdim = p.shape[1]
    tm = min(ROW_TILE, t)
    row = lambda i, f: (i, 0)
    const = lambda i, f: (0, 0)
    return pl.pallas_call(
        functools.partial(_ffn_ple_kernel, final=final),
        grid=(t // tm, dff // tf),
        in_specs=[pl.BlockSpec((tm, d), row),
                  pl.BlockSpec((1, d), const),
                  pl.BlockSpec((d, tf), lambda i, f: (0, f)),
                  pl.BlockSpec((d, tf), lambda i, f: (0, f)),
                  pl.BlockSpec((tf, d), lambda i, f: (f, 0)),
                  pl.BlockSpec((tm, pdim), row),
                  pl.BlockSpec((1, d), const),
                  pl.BlockSpec((d, d), const),
                  pl.BlockSpec((pdim, d), const),
                  pl.BlockSpec((1, d), const)],
        out_specs=pl.BlockSpec((tm, d), row),
        out_shape=jax.ShapeDtypeStruct((t, d), F32),
        scratch_shapes=[pltpu.VMEM((tm, d), MXU_DTYPE), pltpu.VMEM((tm, d), F32)],
        compiler_params=_cparams(("parallel", "arbitrary")),
        name="ffn_ple",
    )(h, nf, w1, w3, w2, p, npl, g, pp, fn)


CONV_HALO = 16
CONV_ROWS = 256


def _norm_linear_conv_kernel(xp_ref, x_ref, xn_ref, nw_ref, w_ref, b_ref, cw_ref, cb_ref, *rest,
                             taps, tiles_per_seq, epilogue):
    *out_refs, xs_ref = rest
    tm = x_ref.shape[0]
    i = pl.program_id(0)

    @pl.when(pl.program_id(1) == 0)
    def _():
        nw = nw_ref[...]
        xs_ref[0:CONV_HALO, :] = _rms(xp_ref[...], nw).astype(xs_ref.dtype)
        xs_ref[CONV_HALO:CONV_HALO + tm, :] = _rms(x_ref[...], nw).astype(xs_ref.dtype)
        xs_ref[CONV_HALO + tm:, :] = _rms(xn_ref[...], nw).astype(xs_ref.dtype)

    first = (i % tiles_per_seq) == 0
    last = (i % tiles_per_seq) == tiles_per_seq - 1
    rows = min(CONV_ROWS, tm)
    nchunks = tm // rows
    for r in range(nchunks):
        base = r * rows
        win = jnp.dot(xs_ref[base:base + rows + 2 * CONV_HALO, :], w_ref[...],
                      preferred_element_type=F32) + b_ref[...]
        top, bottom = win[:CONV_HALO], win[CONV_HALO + rows:]
        if r == 0:
            top = jnp.where(first, 0.0, top)
        if r == nchunks - 1:
            bottom = jnp.where(last, 0.0, bottom)
        win = jnp.concatenate([top, win[CONV_HALO:CONV_HALO + rows], bottom], axis=0)
        acc = cb_ref[...]
        for k in range(taps):
            shift = (taps // 2 - k) % win.shape[0]
            moved = pltpu.roll(win, shift, 0) if shift else win
            acc = acc + cw_ref[k:k + 1, :] * moved[CONV_HALO:CONV_HALO + rows, :]
        epilogue(out_refs, slice(base, base + rows), acc)


def _silu_epilogue(out_refs, rows, acc):
    out_refs[0][rows, :] = _silu(acc)


def _hyena_epilogue(out_refs, rows, acc):
    c = acc.shape[1] // 3
    out_refs[0][rows, :] = acc[:, :c]
    out_refs[1][rows, :] = acc[:, 2 * c:] * acc[:, c:2 * c]


def norm_linear_conv(x, seq, nw, w, b, conv_w, conv_b, col_tile, epilogue, out_widths):
    t, d = x.shape
    n = w.shape[1]
    taps = conv_w.shape[0]
    tm = min(PROJ_ROW_TILE, seq)
    per_tile = tm // CONV_HALO
    last_halo = t // CONV_HALO - 1
    return pl.pallas_call(
        functools.partial(_norm_linear_conv_kernel, taps=taps, tiles_per_seq=seq // tm, epilogue=epilogue),
        grid=(t // tm, n // col_tile),
        in_specs=[pl.BlockSpec((CONV_HALO, d), lambda i, j: (jnp.maximum(i * per_tile - 1, 0), 0)),
                  pl.BlockSpec((tm, d), lambda i, j: (i, 0)),
                  pl.BlockSpec((CONV_HALO, d), lambda i, j: (jnp.minimum((i + 1) * per_tile, last_halo), 0)),
                  pl.BlockSpec((1, d), lambda i, j: (0, 0)),
                  pl.BlockSpec((d, col_tile), lambda i, j: (0, j)),
                  pl.BlockSpec((1, col_tile), lambda i, j: (0, j)),
                  pl.BlockSpec((taps, col_tile), lambda i, j: (0, j)),
                  pl.BlockSpec((1, col_tile), lambda i, j: (0, j))],
        out_specs=[pl.BlockSpec((tm, ow), lambda i, j: (i, j)) for ow in out_widths],
        out_shape=[jax.ShapeDtypeStruct((t, n // col_tile * ow), F32) for ow in out_widths],
        scratch_shapes=[pltpu.VMEM((tm + 2 * CONV_HALO, d), MXU_DTYPE)],
        compiler_params=_cparams(("parallel", "arbitrary")),
        name="norm_linear_conv",
    )(x, x, x, nw, w, b, conv_w, conv_b)


def _tri(n, reverse):
    row = lax.broadcasted_iota(jnp.int32, (n, n), 0)
    col = lax.broadcasted_iota(jnp.int32, (n, n), 1)
    return (row <= col) if reverse else (row >= col)


def _ssd_decays(dt_ref, dtb_ref, alog_ref, *, heads, lane0, reverse):
    t = SSD_CHUNK
    dt_all = _softplus(dt_ref[...] + dtb_ref[...])
    da_all = dt_all * (-jnp.exp(alog_ref[...]))
    da = da_all[:, lane0:lane0 + heads]
    dt_t = dt_all.T[lane0:lane0 + heads, :]
    da_t = da_all.T[lane0:lane0 + heads, :]
    acum = _mask_times(_tri(t, reverse), da)
    acum_t = _times_mask(da_t, _tri(t, not reverse))
    tot_c = acum_t[:, 0:1] if reverse else acum_t[:, t - 1:t]
    w_t = jnp.exp(tot_c - acum_t) * dt_t
    src_t = acum_t - jnp.log(dt_t)
    etot = jnp.exp(jnp.sum(da, axis=0, keepdims=True))
    return acum, src_t, w_t, etot


def _ssd_group(g, xs_ref, b_ref, c_ref, st_ref, y_ref, decays, *, reverse):
    t = SSD_CHUNK
    acum, src_t, w_t, etot = decays
    per_group = xs_ref.shape[1] // SSD_HEADDIM // SSD_GROUPS
    gw = per_group * SSD_HEADDIM
    tri = _tri(t, reverse)
    first = lax.broadcasted_iota(jnp.int32, (t, SSD_PAIR), 1) < SSD_HEADDIM
    bg_t = b_ref[:, g * SSD_STATE:(g + 1) * SSD_STATE].T
    cg = c_ref[:, g * SSD_STATE:(g + 1) * SSD_STATE].astype(MXU_DTYPE)
    cb = _dot(cg, bg_t)
    st = st_ref[g]
    y_intra, e_acum, st_new = [], [], []
    for pr in range(per_group // 2):
        h0 = g * per_group + 2 * pr
        x_pair = xs_ref[:, h0 * SSD_HEADDIM:(h0 + 2) * SSD_HEADDIM]
        rhs = jnp.concatenate([jnp.where(first, x_pair, 0.0),
                               jnp.where(first, 0.0, x_pair)], axis=0).astype(MXU_DTYPE)
        cols = [jnp.broadcast_to(acum[:, h:h + 1], (t, SSD_PAIR)) for h in (h0, h0 + 1)]
        scores = [cb * jnp.exp(jnp.where(tri, cols[k] - src_t[h0 + k:h0 + k + 1, :], -jnp.inf)) for k in range(2)]
        y_intra.append(_dot(jnp.concatenate(scores, axis=1), rhs))
        e_acum.append(jnp.where(first, jnp.exp(cols[0]), jnp.exp(cols[1])))
        e_tot = jnp.where(first[0:1], jnp.broadcast_to(etot[:, h0:h0 + 1], (1, SSD_PAIR)),
                          jnp.broadcast_to(etot[:, h0 + 1:h0 + 2], (1, SSD_PAIR)))
        b_weighted = jnp.concatenate([bg_t * w_t[h0:h0 + 1, :], bg_t * w_t[h0 + 1:h0 + 2, :]], axis=1)
        st_new.append(st[:, pr * SSD_PAIR:(pr + 1) * SSD_PAIR] * e_tot + _dot(b_weighted, rhs))
    y_ref[:, g * gw:(g + 1) * gw] = (_dot(cg, st) * jnp.concatenate(e_acum, axis=1)
                                     + jnp.concatenate(y_intra, axis=1))
    st_ref[g] = jnp.concatenate(st_new, axis=1)


def _ssd_scan_kernel(xf_ref, bf_ref, cf_ref, dtf_ref, xb_ref, bb_ref, cb_ref, dtb_ref,
                     bias_ref, alog_ref, yf_ref, yb_ref, stf_ref, stb_ref):
    @pl.when(pl.program_id(1) == 0)
    def _():
        stf_ref[...] = jnp.zeros_like(stf_ref)
        stb_ref[...] = jnp.zeros_like(stb_ref)

    heads = xf_ref.shape[1] // SSD_HEADDIM
    dec_f = _ssd_decays(dtf_ref, bias_ref, alog_ref, heads=heads, lane0=0, reverse=False)
    dec_b = _ssd_decays(dtb_ref, bias_ref, alog_ref, heads=heads, lane0=heads, reverse=True)
    for g in range(SSD_GROUPS):
        _ssd_group(g, xf_ref, bf_ref, cf_ref, stf_ref, yf_ref, dec_f, reverse=False)
        _ssd_group(g, xb_ref, bb_ref, cb_ref, stb_ref, yb_ref, dec_b, reverse=True)


def ssd_scan(xbc, dt_raw, dt_bias, a_log):
    bsz, seq, width = xbc.shape
    gn = SSD_GROUPS * SSD_STATE
    d_inner = width - 2 * gn
    t = SSD_CHUNK
    nc = seq // t
    xblk = d_inner // gn
    fwd = lambda b, c: (b, c)
    bwd = lambda b, c: (b, nc - 1 - c)

    def specs(pos):
        return [pl.BlockSpec((None, t, d_inner), lambda b, c: (*pos(b, c), 0)),
                pl.BlockSpec((None, t, gn), lambda b, c: (*pos(b, c), xblk)),
                pl.BlockSpec((None, t, gn), lambda b, c: (*pos(b, c), xblk + 1)),
                pl.BlockSpec((None, t, LANES), lambda b, c: (*pos(b, c), 0))]

    const = pl.BlockSpec((1, LANES), lambda b, c: (0, 0))
    state = pltpu.VMEM((SSD_GROUPS, SSD_STATE, d_inner // SSD_GROUPS), F32)
    return pl.pallas_call(
        _ssd_scan_kernel,
        grid=(bsz, nc),
        in_specs=specs(fwd) + specs(bwd) + [const, const],
        out_specs=[pl.BlockSpec((None, t, d_inner), lambda b, c: (*fwd(b, c), 0)),
                   pl.BlockSpec((None, t, d_inner), lambda b, c: (*bwd(b, c), 0))],
        out_shape=[jax.ShapeDtypeStruct((bsz, seq, d_inner), F32)] * 2,
        scratch_shapes=[state, state],
        compiler_params=_cparams(("parallel", "arbitrary")),
        name="ssd_scan",
    )(xbc, xbc, xbc, dt_raw, xbc, xbc, xbc, dt_raw, dt_bias, a_log)


def _ssd_out_kernel(yf_ref, yb_ref, xs_ref, z_ref, dsk_ref, nw_ref, w_ref, res_ref, o_ref):
    y = (yf_ref[...] + yb_ref[...] + xs_ref[...] * dsk_ref[...]) * _silu(z_ref[...])
    gw = y.shape[1] // SSD_GROUPS
    y = jnp.concatenate([_rms(y[:, g * gw:(g + 1) * gw], nw_ref[:, g * gw:(g + 1) * gw])
                         for g in range(SSD_GROUPS)], axis=1)
    o_ref[...] = res_ref[...] + _dot(y, w_ref[...])


def ssd_out(y_f, y_b, xbc, zx, d_skip, norm_w, out_w, res):
    t, d_inner = y_f.shape
    d = out_w.shape[1]
    tm = min(ROW_TILE, t)
    row = lambda i: (i, 0)
    const = lambda i: (0, 0)
    return pl.pallas_call(
        _ssd_out_kernel,
        grid=(t // tm,),
        in_specs=[pl.BlockSpec((tm, d_inner), row), pl.BlockSpec((tm, d_inner), row),
                  pl.BlockSpec((tm, d_inner), row), pl.BlockSpec((tm, d_inner), row),
                  pl.BlockSpec((1, d_inner), const), pl.BlockSpec((1, d_inner), const),
                  pl.BlockSpec((d_inner, d), const), pl.BlockSpec((tm, d), row)],
        out_specs=pl.BlockSpec((tm, d), row),
        out_shape=jax.ShapeDtypeStruct((t, d), F32),
        compiler_params=_cparams(("parallel",)),
        name="ssd_out",
    )(y_f, y_b, xbc, zx, d_skip, norm_w, out_w, res)


def _pad_rows(x, r0, total):
    parts = []
    if r0 > 0:
        parts.append(jnp.zeros((r0, x.shape[1]), x.dtype))
    parts.append(x)
    if r0 + x.shape[0] < total:
        parts.append(jnp.zeros((total - r0 - x.shape[0], x.shape[1]), x.dtype))
    return jnp.concatenate(parts, axis=0) if len(parts) > 1 else x


def _gla_chunk(q_ref, k_ref, v_ref, o_ref, bc, states, rs, *, reverse):
    ch = GLA_CHUNK
    kd = q_ref.shape[1]
    dk = kd // GLA_HEADS
    dv = v_ref.shape[1] // GLA_HEADS
    nsub = ch // GLA_SUB
    tri = _tri(ch, reverse)
    eye = lax.broadcasted_iota(jnp.int32, (dk, dk), 0) == lax.broadcasted_iota(jnp.int32, (dk, dk), 1)
    tot = bc[0:1] if reverse else bc[ch - 1:ch]
    q = q_ref[rs, :] * (dk ** -0.5)
    k = k_ref[rs, :]
    v = v_ref[rs, :].astype(MXU_DTYPE)
    q_in = q * jnp.exp(bc)
    k_st = k * jnp.exp(tot - bc)
    e_tot = jnp.exp(tot)
    new_states = []
    for h in range(GLA_HEADS):
        ks = slice(h * dk, (h + 1) * dk)
        vs = slice(h * dv, (h + 1) * dv)
        bch, qh, kh = bc[:, ks], q[:, ks], k[:, ks]
        q_segs, k_segs = [], []
        for i in range(nsub):
            l0, l1 = i * GLA_SUB, (i + 1) * GLA_SUB
            if reverse:
                s0, s1 = l0, ch
                ref = bch[l1:l1 + 1] if l1 < ch else 0.0
            else:
                s0, s1 = 0, l1
                ref = bch[l0 - 1:l0] if l0 > 0 else 0.0
            q_segs.append(_pad_rows(qh[l0:l1] * jnp.exp(bch[l0:l1] - ref), l0, ch))
            k_segs.append(_pad_rows(kh[s0:s1] * jnp.exp(ref - bch[s0:s1]), s0, ch))
        att = _dot_nt(jnp.concatenate(q_segs, axis=1), jnp.concatenate(k_segs, axis=1))
        att = jnp.where(tri, att, 0.0)
        st = states[h]
        o_ref[rs, vs] = _dot(jnp.concatenate([q_in[:, ks], att], axis=1),
                             jnp.concatenate([st.astype(MXU_DTYPE), v[:, vs]], axis=0))
        e_col = jnp.sum(jnp.where(eye, jnp.broadcast_to(e_tot[:, ks], (dk, dk)), 0.0), axis=1, keepdims=True)
        new_states.append(st * e_col + _dot_tn(k_st[:, ks], v[:, vs]))
    return new_states


def _gla_gate_sums(gl_ref, wgk_ref, bgk_ref, cols, reverse):
    x = _dot(gl_ref[...], wgk_ref[:, cols]) + bgk_ref[:, cols]
    g = -_softplus(-x) * (1.0 / GLA_GATE_NORMALIZER)
    rows = g.shape[0]
    row = lax.broadcasted_iota(jnp.int32, (rows, rows), 0)
    col = lax.broadcasted_iota(jnp.int32, (rows, rows), 1)
    same_chunk = (row // GLA_CHUNK) == (col // GLA_CHUNK)
    return _mask_times(same_chunk & ((row <= col) if reverse else (row >= col)), g)


def _gla_scan_kernel(qf_ref, kf_ref, vf_ref, glf_ref, qb_ref, kb_ref, vb_ref, glb_ref, wgk_ref, bgk_ref,
                     of_ref, ob_ref, stf_ref, stb_ref):
    @pl.when(pl.program_id(1) == 0)
    def _():
        stf_ref[...] = jnp.zeros_like(stf_ref)
        stb_ref[...] = jnp.zeros_like(stb_ref)

    ch = GLA_CHUNK
    rows, kd = qf_ref.shape
    nchunks = rows // ch
    g_f = _gla_gate_sums(glf_ref, wgk_ref, bgk_ref, slice(0, kd), reverse=False)
    g_b = _gla_gate_sums(glb_ref, wgk_ref, bgk_ref, slice(kd, 2 * kd), reverse=True)
    st_f = [stf_ref[h] for h in range(GLA_HEADS)]
    st_b = [stb_ref[h] for h in range(GLA_HEADS)]
    for i in range(nchunks):
        rf = slice(i * ch, (i + 1) * ch)
        rb = slice((nchunks - 1 - i) * ch, (nchunks - i) * ch)
        st_f = _gla_chunk(qf_ref, kf_ref, vf_ref, of_ref, g_f[rf], st_f, rf, reverse=False)
        st_b = _gla_chunk(qb_ref, kb_ref, vb_ref, ob_ref, g_b[rb], st_b, rb, reverse=True)
    for h in range(GLA_HEADS):
        stf_ref[h] = st_f[h]
        stb_ref[h] = st_b[h]


def gla_scan(qkvg, gl, wgk, bgk, key_dim, value_dim):
    bsz, seq, _ = qkvg.shape
    ch = min(GLA_BLOCK, seq)
    nc = seq // ch
    fwd = lambda b, c: (b, c)
    bwd = lambda b, c: (b, nc - 1 - c)

    def specs(pos):
        return [pl.BlockSpec((None, ch, key_dim), lambda b, c: (*pos(b, c), 0)),
                pl.BlockSpec((None, ch, key_dim), lambda b, c: (*pos(b, c), 1)),
                pl.BlockSpec((None, ch, value_dim), lambda b, c: (*pos(b, c), (2 * key_dim) // value_dim)),
                pl.BlockSpec((None, ch, LANES), lambda b, c: (*pos(b, c), 0))]

    state = pltpu.VMEM((GLA_HEADS, key_dim // GLA_HEADS, value_dim // GLA_HEADS), F32)
    return pl.pallas_call(
        _gla_scan_kernel,
        grid=(bsz, nc),
        in_specs=specs(fwd) + specs(bwd) + [pl.BlockSpec((LANES, 2 * key_dim), lambda b, c: (0, 0)),
                                            pl.BlockSpec((1, 2 * key_dim), lambda b, c: (0, 0))],
        out_specs=[pl.BlockSpec((None, ch, value_dim), lambda b, c: (*fwd(b, c), 0)),
                   pl.BlockSpec((None, ch, value_dim), lambda b, c: (*bwd(b, c), 0))],
        out_shape=[jax.ShapeDtypeStruct((bsz, seq, value_dim), F32)] * 2,
        scratch_shapes=[state, state],
        compiler_params=_cparams(("parallel", "arbitrary")),
        name="gla_scan",
    )(qkvg, qkvg, qkvg, gl, qkvg, qkvg, qkvg, gl, wgk, bgk)


def _gla_out_kernel(of_ref, ob_ref, g_ref, nw_ref, w_ref, res_ref, o_ref):
    o = of_ref[...] + ob_ref[...]
    dv = o.shape[1] // GLA_HEADS
    o = jnp.concatenate([_rms(o[:, h * dv:(h + 1) * dv], nw_ref[...]) for h in range(GLA_HEADS)], axis=1)
    o_ref[...] = res_ref[...] + _dot(o * _silu(g_ref[...]), w_ref[...])


def gla_out(o_f, o_b, qkvg, norm_w, out_w, res, g_col):
    t, vd = o_f.shape
    d = out_w.shape[1]
    tm = min(ROW_TILE, t)
    row = lambda i: (i, 0)
    const = lambda i: (0, 0)
    return pl.pallas_call(
        _gla_out_kernel,
        grid=(t // tm,),
        in_specs=[pl.BlockSpec((tm, vd), row), pl.BlockSpec((tm, vd), row),
                  pl.BlockSpec((tm, vd), lambda i: (i, g_col // vd)),
                  pl.BlockSpec((1, vd // GLA_HEADS), const),
                  pl.BlockSpec((vd, d), const), pl.BlockSpec((tm, d), row)],
        out_specs=pl.BlockSpec((tm, d), row),
        out_shape=jax.ShapeDtypeStruct((t, d), F32),
        compiler_params=_cparams(("parallel",)),
        name="gla_out",
    )(o_f, o_b, qkvg, norm_w, out_w, res)


def _hy_filter_kernel(w1_ref, b1_ref, w2_ref, b2_ref, w3_ref, freq_ref, delta_ref, o_ref, *, seq):
    tr = o_ref.shape[0]
    r = pl.program_id(1) * tr + lax.broadcasted_iota(jnp.int32, (tr, 1), 0)
    pos = jnp.where(r < seq, r, 2 * seq - r).astype(F32)
    t = pos / (seq - 1.0)
    w = (2.0 * math.pi) * pos / seq
    band = lax.broadcasted_iota(jnp.int32, (1, HY_BANDS), 1).astype(F32)
    bands = 1e-4 + band * ((HY_BANDS - 1 - 1e-4) / (HY_BANDS - 1))
    zw1 = (t * w1_ref[0:1, :] + _dot_exact(jnp.cos(bands * w), w1_ref[1:1 + HY_BANDS, :])
           + _dot_exact(-jnp.sin(bands * w), w1_ref[1 + HY_BANDS:1 + 2 * HY_BANDS, :]))
    freq = freq_ref[...]
    hid = jnp.sin(freq * (zw1 + b1_ref[...]))
    hid = jnp.sin(freq * (_dot_exact(hid, w2_ref[...]) + b2_ref[...]))
    filt = _dot_exact(hid, w3_ref[...]) * jnp.exp(-t * delta_ref[...])
    o_ref[...] = jnp.where(r == seq, 0.0, filt)


def hy_filter(seq, w1, b1, w2, b2, w3, freq, width, tr=512):
    n = 2 * seq
    tr = min(tr, seq)
    tc = width
    emb, hidden = w1.shape
    min_decay = math.log(HY_DECAY_TARGET) / HY_SLOW_DECAY_PCT
    max_decay = math.log(HY_DECAY_TARGET) / HY_FAST_DECAY_PCT
    deltas = jnp.abs(jnp.linspace(min_decay, max_decay, width, dtype=F32))[None, :]
    nct = width // tc
    half = seq // tr
    const = lambda j, i: (0, 0)
    return pl.pallas_call(
        functools.partial(_hy_filter_kernel, seq=seq),
        grid=(nct, n // tr),
        in_specs=[pl.BlockSpec((emb, hidden), const), pl.BlockSpec((1, hidden), const),
                  pl.BlockSpec((hidden, hidden), const), pl.BlockSpec((1, hidden), const),
                  pl.BlockSpec((hidden, tc), lambda j, i: (0, j + nct * (i // half))),
                  pl.BlockSpec((1, hidden), const),
                  pl.BlockSpec((1, tc), lambda j, i: (0, j))],
        out_specs=pl.BlockSpec((tr, tc), lambda j, i: (i, j)),
        out_shape=jax.ShapeDtypeStruct((n, width), F32),
        compiler_params=_cparams(("parallel", "parallel")),
        name="hy_filter",
    )(w1, b1, w2, b2, w3, freq, deltas)


def _stack_complex(m):
    return np.concatenate([np.concatenate([m.real, -m.imag], axis=-1),
                           np.concatenate([m.imag, m.real], axis=-1)], axis=-2)


@functools.lru_cache(maxsize=None)
def _dft_tables(n, n1):
    n2 = n // n1
    h1 = n1 // 2
    ang = -2.0 * np.pi / n
    k1 = np.arange(n1)[None, :, None]
    nn1 = np.arange(n1)[None, None, :]
    nn2 = np.arange(n2)[:, None, None]
    g1_full = np.exp(1j * ang * k1 * (n2 * nn1 + nn2))
    g1 = _stack_complex(g1_full[:, :, :h1])
    g1_real = np.concatenate([g1_full.real, g1_full.imag], axis=-2)
    kk2 = np.arange(n2)[:, None]
    f2 = np.exp(-2j * np.pi * kk2 * np.arange(n2)[None, :] / n2)
    f2s = _stack_complex(f2)
    f3s = _stack_complex(np.conj(f2))
    out1 = np.arange(h1)[None, :, None]
    g4 = np.exp(-1j * ang * (n2 * out1 + nn2) * np.arange(n1)[None, None, :]) / n
    g4s = _stack_complex(g4)
    cast = lambda a: jnp.asarray(a, dtype=F32).astype(MXU_DTYPE)
    return cast(g1), cast(g1_real), cast(f2s), cast(f3s), cast(g4s)


FFT_UNROLL = 8
FFT_SLABS = 4
FFT_PAD = 8


def _load_complex_strided(ref, start, rows, stride):
    return jnp.concatenate([ref[0, pl.ds(start, rows, stride=stride), :],
                            ref[1, pl.ds(start, rows, stride=stride), :]], axis=0)


def _store_complex_slab(ref, base, rows, out):
    ref[0, pl.ds(base, rows), :] = out[:rows]
    ref[1, pl.ds(base, rows), :] = out[rows:]


def _shared_matrix_stage(load, store, mat_ref, count):
    def body(jj, carry):
        j0 = jj * FFT_SLABS
        x = jnp.concatenate([load(j0 + u) for u in range(FFT_SLABS)], axis=1).astype(MXU_DTYPE)
        out = jnp.dot(mat_ref[...], x, preferred_element_type=F32)
        for u in range(FFT_SLABS):
            store(j0 + u, out[:, u * LANES:(u + 1) * LANES])
        return carry

    lax.fori_loop(0, count // FFT_SLABS, body, 0, unroll=2)


def _per_slab_matrix_stage(load, store, mats_ref, count):
    def body(j, carry):
        store(j, jnp.dot(mats_ref[j], load(j).astype(MXU_DTYPE), preferred_element_type=F32))
        return carry

    lax.fori_loop(0, count, body, 0, unroll=FFT_UNROLL)


def _slab_base(j, rows):
    return pl.multiple_of(j * (rows + FFT_PAD), 8)


def _scratch_rows(n, n1):
    n2 = n // n1
    return max(n2 * (n1 + FFT_PAD), n1 * (n2 + FFT_PAD))


def _hy_spectrum_kernel(filt_ref, g1r_ref, f2s_ref, hf_ref, a_ref, *, n1):
    n = filt_ref.shape[0]
    n2 = n // n1
    _per_slab_matrix_stage(
        lambda j: filt_ref[pl.ds(j, n1, stride=n2), :],
        lambda j, out: _store_complex_slab(a_ref, _slab_base(j, n1), n1, out),
        g1r_ref, n2)
    _shared_matrix_stage(
        lambda j: _load_complex_strided(a_ref, j, n2, n1 + FFT_PAD),
        lambda j, out: _store_complex_slab(hf_ref, pl.multiple_of(j * n2, n2), n2, out),
        f2s_ref, n1)


def hy_spectrum(filt):
    n, c = filt.shape
    n1 = FFT_N1
    _, g1r, f2s, _, _ = _dft_tables(n, n1)
    return pl.pallas_call(
        functools.partial(_hy_spectrum_kernel, n1=n1),
        grid=(c // LANES,),
        in_specs=[pl.BlockSpec((n, LANES), lambda j: (0, j)),
                  pl.BlockSpec(g1r.shape, lambda j: (0, 0, 0)),
                  pl.BlockSpec(f2s.shape, lambda j: (0, 0))],
        out_specs=pl.BlockSpec((2, n, LANES), lambda j: (0, 0, j)),
        out_shape=jax.ShapeDtypeStruct((2, n, c), F32),
        scratch_shapes=[pltpu.VMEM((2, _scratch_rows(n, n1), LANES), F32)],
        compiler_params=_cparams(("parallel",)),
        name="hy_spectrum",
    )(filt, g1r, f2s)


def _hy_fftconv_kernel(v_ref, hf_ref, g1_ref, f2s_ref, f3s_ref, g4s_ref, o_ref, a_ref, y_ref, *, n1):
    seq = v_ref.shape[1]
    n = 2 * seq
    n2 = n // n1
    h1 = n1 // 2

    def multiply_by_spectrum(j, out):
        base = pl.multiple_of(j * n2, n2)
        xr, xi = out[:n2], out[n2:]
        hr = hf_ref[0, pl.ds(base, n2), :]
        hi = hf_ref[1, pl.ds(base, n2), :]
        y_ref[0, pl.ds(base, n2), :] = xr * hr - xi * hi
        y_ref[1, pl.ds(base, n2), :] = xr * hi + xi * hr

    def stage_input(m, carry):
        for comp in range(2):
            y_ref[comp, pl.ds(_slab_base(m, n2), n2), :] = v_ref[comp, pl.ds(pl.multiple_of(m * n2, n2), n2), :]
        return carry

    lax.fori_loop(0, h1, stage_input, 0)

    def stage_output(j, out):
        _store_complex_slab(y_ref, _slab_base(j, h1), h1, out)

    def unstage_output(m, carry):
        for comp in range(2):
            o_ref[comp, pl.ds(pl.multiple_of(m * n2, n2), n2), :] = y_ref[comp, pl.ds(m, n2, stride=h1 + FFT_PAD), :]
        return carry

    _per_slab_matrix_stage(
        lambda j: _load_complex_strided(y_ref, j, h1, n2 + FFT_PAD),
        lambda j, out: _store_complex_slab(a_ref, _slab_base(j, n1), n1, out),
        g1_ref, n2)
    _shared_matrix_stage(lambda j: _load_complex_strided(a_ref, j, n2, n1 + FFT_PAD), multiply_by_spectrum,
                         f2s_ref, n1)
    _shared_matrix_stage(
        lambda j: jnp.concatenate([y_ref[0, pl.ds(pl.multiple_of(j * n2, n2), n2), :],
                                   y_ref[1, pl.ds(pl.multiple_of(j * n2, n2), n2), :]], axis=0),
        lambda j, out: _store_complex_slab(a_ref, _slab_base(j, n2), n2, out),
        f3s_ref, n1)
    _per_slab_matrix_stage(lambda j: _load_complex_strided(a_ref, j, n1, n2 + FFT_PAD), stage_output, g4s_ref, n2)
    lax.fori_loop(0, h1, unstage_output, 0)


def hy_fftconv(vx, hf):
    bsz, seq, c = vx.shape
    n = 2 * seq
    n1 = FFT_N1
    g1, _, f2s, f3s, g4s = _dft_tables(n, n1)
    once = pl.Buffered(1)
    tab3 = lambda a: pl.BlockSpec(a.shape, lambda j, p: (0, 0, 0), pipeline_mode=once)
    tab2 = lambda a: pl.BlockSpec(a.shape, lambda j, p: (0, 0), pipeline_mode=once)
    pair = pl.BlockSpec((None, 2, seq, LANES), lambda j, p: (p, 0, 0, j))
    out = pl.pallas_call(
        functools.partial(_hy_fftconv_kernel, n1=n1),
        grid=(c // LANES, bsz // 2),
        in_specs=[pair,
                  pl.BlockSpec((2, n, LANES), lambda j, p: (0, 0, j), pipeline_mode=once),
                  tab3(g1), tab2(f2s), tab2(f3s), tab3(g4s)],
        out_specs=pair,
        out_shape=jax.ShapeDtypeStruct((bsz // 2, 2, seq, c), F32),
        scratch_shapes=[pltpu.VMEM((2, _scratch_rows(n, n1), LANES), F32), pltpu.VMEM((2, n, LANES), F32)],
        compiler_params=_cparams(("parallel", "parallel")),
        name="hy_fftconv",
    )(vx.reshape(bsz // 2, 2, seq, c), hf, g1, f2s, f3s, g4s)
    return out.reshape(bsz, seq, c)


def _hy_out_kernel(conv_ref, vx_ref, x0_ref, skip_ref, w_ref, b_ref, res_ref, o_ref):
    y = (conv_ref[...] + vx_ref[...] * skip_ref[...]) * x0_ref[...]
    o_ref[...] = res_ref[...] + _dot(y, w_ref[...]) + b_ref[...]


def hy_out(conv, vx, x0c, skip, out_w, out_b, res):
    t, c = vx.shape
    d = out_w.shape[1]
    tm = min(ROW_TILE, t)
    row = lambda i: (i, 0)
    const = lambda i: (0, 0)
    return pl.pallas_call(
        _hy_out_kernel,
        grid=(t // tm,),
        in_specs=[pl.BlockSpec((tm, c), row), pl.BlockSpec((tm, c), row), pl.BlockSpec((tm, c), row),
                  pl.BlockSpec((1, c), const), pl.BlockSpec((c, d), const), pl.BlockSpec((1, d), const),
                  pl.BlockSpec((tm, d), row)],
        out_specs=pl.BlockSpec((tm, d), row),
        out_shape=jax.ShapeDtypeStruct((t, d), F32),
        compiler_params=_cparams(("parallel",)),
        name="hy_out",
    )(conv, vx, x0c, skip, out_w, out_b, res)


def _pad_cols(a, width):
    return jnp.pad(a, ((0, 0), (0, width - a.shape[1])))


def _col_tile(n):
    for tile in (1024, 768, 512, 256, 128):
        if n % tile == 0:
            return tile
    raise ValueError(f"unsupported projection width {n}")


def mamba2_mixer(h, bsz, norm_w, in_w, conv_w, conv_b, dt_bias, a_log, d_skip, gn_w, out_w):
    t, d = h.shape
    seq = t // bsz
    heads = a_log.shape[1]
    d_inner = heads * SSD_HEADDIM
    conv_dim = conv_w.shape[1]
    main = d_inner + conv_dim
    w_dt = _pad_cols(in_w[:, main:], LANES).astype(MXU_DTYPE)
    z, dt_raw = norm_linear(h, norm_w, in_w[:, :d_inner].astype(MXU_DTYPE), w_dt, _col_tile(d_inner))
    (xbc,) = norm_linear_conv(h, seq, norm_w, in_w[:, d_inner:main].astype(MXU_DTYPE), jnp.zeros((1, conv_dim), F32),
                              conv_w, conv_b[None, :], _col_tile(conv_dim), _silu_epilogue, (_col_tile(conv_dim),))
    y_f, y_b = ssd_scan(xbc.reshape(bsz, seq, conv_dim), dt_raw.reshape(bsz, seq, LANES),
                        _pad_cols(dt_bias.reshape(1, 2 * heads), LANES),
                        _pad_cols(a_log.reshape(1, 2 * heads), LANES))
    return ssd_out(y_f.reshape(t, d_inner), y_b.reshape(t, d_inner), xbc, z,
                   jnp.repeat(d_skip, SSD_HEADDIM)[None, :], gn_w[None, :], out_w.astype(MXU_DTYPE), h)


def hyena_mixer(h, bsz, norm_w, in_w, in_b, conv_w, conv_b, f_w1, f_b1, f_w2, f_b2, f_w3, sin_freq, skip,
                out_w, out_b):
    t, d = h.shape
    seq = t // bsz
    width = skip.shape[0]
    sub = HY_PART_TILE
    by_tile = lambda a: a.reshape(a.shape[0], 3, width // sub, sub).transpose(0, 2, 1, 3).reshape(a.shape[0], 3 * width)
    x0c, vx = norm_linear_conv(h, seq, norm_w, by_tile(in_w).astype(MXU_DTYPE), by_tile(in_b[None, :]),
                               by_tile(conv_w), by_tile(conv_b[None, :]), 3 * sub, _hyena_epilogue, (sub, sub))
    filt = hy_filter(seq, f_w1, f_b1[None, :], f_w2, f_b2[None, :], f_w3, sin_freq[None, :], width)
    conv = hy_fftconv(vx.reshape(bsz, seq, width), hy_spectrum(filt))
    return hy_out(conv.reshape(t, width), vx, x0c, skip[None, :], out_w.astype(MXU_DTYPE), out_b[None, :], h)


def gla_mixer(h, bsz, norm_w, in_w, gk_w, gk_b, hn_w, out_w):
    t, d = h.shape
    seq = t // bsz
    rank, key_dim = gk_w.shape[1], gk_w.shape[2]
    value_dim = out_w.shape[0]
    main = 2 * key_dim + 2 * value_dim
    w_gl = _pad_cols(in_w[:, main:], LANES).astype(MXU_DTYPE)
    qkvg, gl = norm_linear(h, norm_w, in_w[:, :main].astype(MXU_DTYPE), w_gl, _col_tile(main))
    wgk = jnp.zeros((LANES, 2 * key_dim), F32)
    wgk = wgk.at[:rank, :key_dim].set(gk_w[0]).at[rank:2 * rank, key_dim:].set(gk_w[1]).astype(MXU_DTYPE)
    o_f, o_b = gla_scan(qkvg.reshape(bsz, seq, main), gl.reshape(bsz, seq, LANES), wgk,
                        gk_b.reshape(1, 2 * key_dim), key_dim, value_dim)
    return gla_out(o_f.reshape(t, value_dim), o_b.reshape(t, value_dim), qkvg, hn_w[None, :],
                   out_w.astype(MXU_DTYPE), h, g_col=2 * key_dim + value_dim)


def kernel(x, p, norm_mix, norm_ffn, norm_ple, ple_gate, ple_proj, ffn_w1, ffn_w3, ffn_w2, final_norm,
           ssd_in_w, ssd_conv_w, ssd_conv_b, ssd_dt_bias, ssd_a_log, ssd_d, ssd_norm, ssd_out_w,
           hy_in_w, hy_in_b, hy_conv_w, hy_conv_b, hy_f_w1, hy_f_b1, hy_f_w2, hy_f_b2, hy_f_w3,
           hy_sin_freq, hy_skip, hy_out_w, hy_out_b,
           gla_in_w, gla_gk_w, gla_gk_b, gla_norm, gla_out_w):
    bsz, seq, d = x.shape
    depth = p.shape[0]
    t = bsz * seq
    h = x.reshape(t, d)
    for i in range(depth):
        kind, j = i % N_MIXERS, i // N_MIXERS
        nw = norm_mix[i][None, :]
        if kind == 0:
            h = mamba2_mixer(h, bsz, nw, ssd_in_w[j], ssd_conv_w[j], ssd_conv_b[j], ssd_dt_bias[j], ssd_a_log[j],
                             ssd_d[j], ssd_norm[j], ssd_out_w[j])
        elif kind == 1:
            h = hyena_mixer(h, bsz, nw, hy_in_w[j], hy_in_b[j], hy_conv_w[j], hy_conv_b[j], hy_f_w1[j], hy_f_b1[j],
                            hy_f_w2[j], hy_f_b2[j], hy_f_w3[j], hy_sin_freq[j], hy_skip[j], hy_out_w[j], hy_out_b[j])
        else:
            h = gla_mixer(h, bsz, nw, gla_in_w[j], gla_gk_w[j], gla_gk_b[j], gla_norm[j], gla_out_w[j])
        h = ffn_ple(h, norm_ffn[i][None, :], ffn_w1[i].astype(MXU_DTYPE), ffn_w3[i].astype(MXU_DTYPE),
                    ffn_w2[i].astype(MXU_DTYPE), p[i].reshape(t, -1), norm_ple[i][None, :],
                    ple_gate[i].astype(MXU_DTYPE), ple_proj[i].astype(MXU_DTYPE), final_norm[None, :],
                    final=(i == depth - 1))
    return h.reshape(bsz, seq, d)
```

```python
import functools
import math

import numpy as np
import jax
import jax.numpy as jnp
from jax import lax
from jax.experimental import pallas as pl
from jax.experimental.pallas import tpu as pltpu

F32 = jnp.float32
MXU_DTYPE = jnp.bfloat16
EXACT = lax.Precision.HIGHEST

NORM_EPS = 1e-6
N_MIXERS = 3

VMEM_LIMIT_BYTES = 56 * 1024 * 1024
ROW_TILE = 512
PROJ_ROW_TILE = 1024
LANES = 128
FFN_SLICE = 512

SSD_HEADDIM = 64
SSD_GROUPS = 8
SSD_STATE = 128
SSD_CHUNK = 128
SSD_PAIR = 2 * SSD_HEADDIM
assert SSD_PAIR == LANES == SSD_CHUNK
GLA_HEADS = 4
GLA_CHUNK = 64
GLA_BLOCK = 256
GLA_SUB = 16
GLA_GATE_RANK = 16
GLA_GATE_NORMALIZER = 16.0
HY_PART_TILE = 256
HY_BANDS = 16
HY_DECAY_TARGET = 1e-2
HY_FAST_DECAY_PCT = 0.3
HY_SLOW_DECAY_PCT = 1.5
FFT_N1 = 64


def _cparams(sem):
    return pltpu.CompilerParams(dimension_semantics=sem, vmem_limit_bytes=VMEM_LIMIT_BYTES)


def _dot(a, b):
    return jnp.dot(a.astype(MXU_DTYPE), b.astype(MXU_DTYPE), preferred_element_type=F32)


def _dot_nt(a, b):
    return lax.dot_general(a.astype(MXU_DTYPE), b.astype(MXU_DTYPE), (((1,), (1,)), ((), ())),
                           preferred_element_type=F32)


def _dot_tn(a, b):
    return lax.dot_general(a.astype(MXU_DTYPE), b.astype(MXU_DTYPE), (((0,), (0,)), ((), ())),
                           preferred_element_type=F32)


def _dot_exact(a, b):
    return jnp.dot(a, b, preferred_element_type=F32, precision=EXACT)


def _split3(x):
    hi = x.astype(MXU_DTYPE)
    rest = x - hi.astype(F32)
    mid = rest.astype(MXU_DTYPE)
    lo = (rest - mid.astype(F32)).astype(MXU_DTYPE)
    return hi, mid, lo


def _mask_times(mask, x):
    m = jnp.where(mask, 1.0, 0.0).astype(MXU_DTYPE)
    return jnp.dot(jnp.concatenate([m, m, m], axis=1), jnp.concatenate(_split3(x), axis=0),
                   preferred_element_type=F32)


def _times_mask(x, mask):
    m = jnp.where(mask, 1.0, 0.0).astype(MXU_DTYPE)
    return jnp.dot(jnp.concatenate(_split3(x), axis=1), jnp.concatenate([m, m, m], axis=0),
                   preferred_element_type=F32)


def _rms(x, w):
    return x * lax.rsqrt(jnp.mean(x * x, axis=-1, keepdims=True) + NORM_EPS) * w


def _sigmoid(x):
    return 1.0 / (1.0 + jnp.exp(-x))


def _silu(x):
    return x * _sigmoid(x)


def _softplus(x):
    return jnp.maximum(x, 0.0) + jnp.log(1.0 + jnp.exp(-jnp.abs(x)))


def _norm_linear_kernel(x_ref, nw_ref, w_ref, ws_ref, o_ref, os_ref, xn_ref):
    @pl.when(pl.program_id(1) == 0)
    def _():
        xn_ref[...] = _rms(x_ref[...], nw_ref[...]).astype(xn_ref.dtype)
        os_ref[...] = jnp.dot(xn_ref[...], ws_ref[...], preferred_element_type=F32)

    o_ref[...] = jnp.dot(xn_ref[...], w_ref[...], preferred_element_type=F32)


def norm_linear(x, nw, w, w_side, col_tile):
    t, d = x.shape
    n = w.shape[1]
    tm = min(PROJ_ROW_TILE, t)
    return pl.pallas_call(
        _norm_linear_kernel,
        grid=(t // tm, n // col_tile),
        in_specs=[pl.BlockSpec((tm, d), lambda i, j: (i, 0)),
                  pl.BlockSpec((1, d), lambda i, j: (0, 0)),
                  pl.BlockSpec((d, col_tile), lambda i, j: (0, j)),
                  pl.BlockSpec((d, LANES), lambda i, j: (0, 0))],
        out_specs=[pl.BlockSpec((tm, col_tile), lambda i, j: (i, j)),
                   pl.BlockSpec((tm, LANES), lambda i, j: (i, 0))],
        out_shape=[jax.ShapeDtypeStruct((t, n), F32), jax.ShapeDtypeStruct((t, LANES), F32)],
        scratch_shapes=[pltpu.VMEM((tm, d), MXU_DTYPE)],
        compiler_params=_cparams(("parallel", "arbitrary")),
        name="norm_linear",
    )(x, nw, w, w_side)


def _ffn_ple_kernel(h_ref, nf_ref, w1_ref, w3_ref, w2_ref, p_ref, npl_ref, g_ref, pp_ref, fn_ref,
                    o_ref, xn_ref, acc_ref, *, final):
    f = pl.program_id(1)

    @pl.when(f == 0)
    def _():
        xn_ref[...] = _rms(h_ref[...], nf_ref[...]).astype(xn_ref.dtype)
        acc_ref[...] = jnp.zeros_like(acc_ref)

    xn = xn_ref[...]
    tf = w1_ref.shape[1]
    total = None
    for c0 in range(0, tf, FFN_SLICE):
        cs = slice(c0, min(c0 + FFN_SLICE, tf))
        a = jnp.dot(xn, w1_ref[:, cs], preferred_element_type=F32)
        b = jnp.dot(xn, w3_ref[:, cs], preferred_element_type=F32)
        part = _dot(_silu(a) * b, w2_ref[cs, :])
        total = part if total is None else total + part
    acc_ref[...] += total

    @pl.when(f == pl.num_programs(1) - 1)
    def _():
        h2 = h_ref[...] + acc_ref[...]
        gate = _sigmoid(_dot(_rms(h2, npl_ref[...]), g_ref[...]))
        h3 = h2 + gate * _dot(p_ref[...], pp_ref[...])
        if final:
            h3 = _rms(h3, fn_ref[...])
        o_ref[...] = h3


def ffn_ple(h, nf, w1, w3, w2, p, npl, g, pp, fn, final):
    t, d = h.shape
    dff = w1.shape[1]
    tf = dff // 2
    pdim = p.shape[1]
    tm = min(ROW_TILE, t)
    row = lambda i, f: (i, 0)
    const = lambda i, f: (0, 0)
    return pl.pallas_call(
        functools.partial(_ffn_ple_kernel, final=final),
        grid=(t // tm, dff // tf),
        in_specs=[pl.BlockSpec((tm, d), row),
                  pl.BlockSpec((1, d), const),
                  pl.BlockSpec((d, tf), lambda i, f: (0, f)),
                  pl.BlockSpec((d, tf), lambda i, f: (0, f)),
                  pl.BlockSpec((tf, d), lambda i, f: (f, 0)),
                  pl.BlockSpec((tm, pdim), row),
                  pl.BlockSpec((1, d), const),
                  pl.BlockSpec((d, d), const),
                  pl.BlockSpec((pdim, d), const),
                  pl.BlockSpec((1, d), const)],
        out_specs=pl.BlockSpec((tm, d), row),
        out_shape=jax.ShapeDtypeStruct((t, d), F32),
        scratch_shapes=[pltpu.VMEM((tm, d), MXU_DTYPE), pltpu.VMEM((tm, d), F32)],
        compiler_params=_cparams(("parallel", "arbitrary")),
        name="ffn_ple",
    )(h, nf, w1, w3, w2, p, npl, g, pp, fn)


CONV_HALO = 16
CONV_ROWS = 256


def _norm_linear_conv_kernel(xp_ref, x_ref, xn_ref, nw_ref, w_ref, b_ref, cw_ref, cb_ref, *rest,
                             taps, tiles_per_seq, epilogue, side):
    ws_ref, rest = (rest[0], rest[1:]) if side else (None, rest)
    *out_refs, xs_ref = rest
    os_ref = out_refs.pop() if side else None
    tm = x_ref.shape[0]
    i = pl.program_id(0)

    @pl.when(pl.program_id(1) == 0)
    def _():
        nw = nw_ref[...]
        xs_ref[0:CONV_HALO, :] = _rms(xp_ref[...], nw).astype(xs_ref.dtype)
        xs_ref[CONV_HALO:CONV_HALO + tm, :] = _rms(x_ref[...], nw).astype(xs_ref.dtype)
        xs_ref[CONV_HALO + tm:, :] = _rms(xn_ref[...], nw).astype(xs_ref.dtype)
        if side:
            os_ref[...] = jnp.dot(xs_ref[CONV_HALO:CONV_HALO + tm, :], ws_ref[...], preferred_element_type=F32)

    first = (i % tiles_per_seq) == 0
    last = (i % tiles_per_seq) == tiles_per_seq - 1
    rows = min(CONV_ROWS, tm)
    nchunks = tm // rows
    for r in range(nchunks):
        base = r * rows
        win = jnp.dot(xs_ref[base:base + rows + 2 * CONV_HALO, :], w_ref[...],
                      preferred_element_type=F32) + b_ref[...]
        top, bottom = win[:CONV_HALO], win[CONV_HALO + rows:]
        if r == 0:
            top = jnp.where(first, 0.0, top)
        if r == nchunks - 1:
            bottom = jnp.where(last, 0.0, bottom)
        win = jnp.concatenate([top, win[CONV_HALO:CONV_HALO + rows], bottom], axis=0)
        acc = cb_ref[...]
        for k in range(taps):
            shift = (taps // 2 - k) % win.shape[0]
            moved = pltpu.roll(win, shift, 0) if shift else win
            acc = acc + cw_ref[k:k + 1, :] * moved[CONV_HALO:CONV_HALO + rows, :]
        epilogue(out_refs, slice(base, base + rows), acc)


def _silu_epilogue(out_refs, rows, acc):
    out_refs[0][rows, :] = _silu(acc)


def _hyena_epilogue(out_refs, rows, acc):
    c = acc.shape[1] // 3
    out_refs[0][rows, :] = acc[:, :c]
    out_refs[1][rows, :] = acc[:, 2 * c:] * acc[:, c:2 * c]


def norm_linear_conv(x, seq, nw, w, b, conv_w, conv_b, col_tile, epilogue, out_widths, w_side=None):
    t, d = x.shape
    n = w.shape[1]
    taps = conv_w.shape[0]
    tm = min(PROJ_ROW_TILE, seq)
    per_tile = tm // CONV_HALO
    last_halo = t // CONV_HALO - 1
    side = w_side is not None
    in_specs = [pl.BlockSpec((CONV_HALO, d), lambda i, j: (jnp.maximum(i * per_tile - 1, 0), 0)),
                pl.BlockSpec((tm, d), lambda i, j: (i, 0)),
                pl.BlockSpec((CONV_HALO, d), lambda i, j: (jnp.minimum((i + 1) * per_tile, last_halo), 0)),
                pl.BlockSpec((1, d), lambda i, j: (0, 0)),
                pl.BlockSpec((d, col_tile), lambda i, j: (0, j)),
                pl.BlockSpec((1, col_tile), lambda i, j: (0, j)),
                pl.BlockSpec((taps, col_tile), lambda i, j: (0, j)),
                pl.BlockSpec((1, col_tile), lambda i, j: (0, j))]
    out_specs = [pl.BlockSpec((tm, ow), lambda i, j: (i, j)) for ow in out_widths]
    out_shape = [jax.ShapeDtypeStruct((t, n // col_tile * ow), F32) for ow in out_widths]
    args = [x, x, x, nw, w, b, conv_w, conv_b]
    if side:
        in_specs.append(pl.BlockSpec((d, LANES), lambda i, j: (0, 0)))
        out_specs.append(pl.BlockSpec((tm, LANES), lambda i, j: (i, 0)))
        out_shape.append(jax.ShapeDtypeStruct((t, LANES), F32))
        args.append(w_side)
    return pl.pallas_call(
        functools.partial(_norm_linear_conv_kernel, taps=taps, tiles_per_seq=seq // tm, epilogue=epilogue, side=side),
        grid=(t // tm, n // col_tile),
        in_specs=in_specs,
        out_specs=out_specs,
        out_shape=out_shape,
        scratch_shapes=[pltpu.VMEM((tm + 2 * CONV_HALO, d), MXU_DTYPE)],
        compiler_params=_cparams(("parallel", "arbitrary")),
        name="norm_linear_conv",
    )(*args)


def _tri(n, reverse):
    row = lax.broadcasted_iota(jnp.int32, (n, n), 0)
    col = lax.broadcasted_iota(jnp.int32, (n, n), 1)
    return (row <= col) if reverse else (row >= col)


def _ssd_decays(dt_ref, dtb_ref, alog_ref, *, heads, lane0, reverse):
    t = SSD_CHUNK
    dt_all = _softplus(dt_ref[...] + dtb_ref[...])
    da_all = dt_all * (-jnp.exp(alog_ref[...]))
    da = da_all[:, lane0:lane0 + heads]
    dt_t = dt_all.T[lane0:lane0 + heads, :]
    da_t = da_all.T[lane0:lane0 + heads, :]
    acum = _mask_times(_tri(t, reverse), da)
    acum_t = _times_mask(da_t, _tri(t, not reverse))
    tot_c = acum_t[:, 0:1] if reverse else acum_t[:, t - 1:t]
    w_t = jnp.exp(tot_c - acum_t) * dt_t
    src_t = acum_t - jnp.log(dt_t)
    etot = jnp.exp(jnp.sum(da, axis=0, keepdims=True))
    return acum, src_t, w_t, etot


def _ssd_group(g, xs_ref, b_ref, c_ref, st_ref, y_ref, decays, *, reverse):
    t = SSD_CHUNK
    acum, src_t, w_t, etot = decays
    per_group = xs_ref.shape[1] // SSD_HEADDIM // SSD_GROUPS
    gw = per_group * SSD_HEADDIM
    tri = _tri(t, reverse)
    first = lax.broadcasted_iota(jnp.int32, (t, SSD_PAIR), 1) < SSD_HEADDIM
    bg_t = b_ref[:, g * SSD_STATE:(g + 1) * SSD_STATE].T
    cg = c_ref[:, g * SSD_STATE:(g + 1) * SSD_STATE].astype(MXU_DTYPE)
    cb = _dot(cg, bg_t)
    st = st_ref[g]
    y_intra, e_acum, st_new = [], [], []
    for pr in range(per_group // 2):
        h0 = g * per_group + 2 * pr
        x_pair = xs_ref[:, h0 * SSD_HEADDIM:(h0 + 2) * SSD_HEADDIM]
        rhs = jnp.concatenate([jnp.where(first, x_pair, 0.0),
                               jnp.where(first, 0.0, x_pair)], axis=0).astype(MXU_DTYPE)
        cols = [jnp.broadcast_to(acum[:, h:h + 1], (t, SSD_PAIR)) for h in (h0, h0 + 1)]
        scores = [cb * jnp.exp(jnp.where(tri, cols[k] - src_t[h0 + k:h0 + k + 1, :], -jnp.inf)) for k in range(2)]
        y_intra.append(_dot(jnp.concatenate(scores, axis=1), rhs))
        e_acum.append(jnp.where(first, jnp.exp(cols[0]), jnp.exp(cols[1])))
        e_tot = jnp.where(first[0:1], jnp.broadcast_to(etot[:, h0:h0 + 1], (1, SSD_PAIR)),
                          jnp.broadcast_to(etot[:, h0 + 1:h0 + 2], (1, SSD_PAIR)))
        b_weighted = jnp.concatenate([bg_t * w_t[h0:h0 + 1, :], bg_t * w_t[h0 + 1:h0 + 2, :]], axis=1)
        st_new.append(st[:, pr * SSD_PAIR:(pr + 1) * SSD_PAIR] * e_tot + _dot(b_weighted, rhs))
    y_ref[:, g * gw:(g + 1) * gw] = (_dot(cg, st) * jnp.concatenate(e_acum, axis=1)
                                     + jnp.concatenate(y_intra, axis=1))
    st_ref[g] = jnp.concatenate(st_new, axis=1)


def _ssd_scan_kernel(xf_ref, bf_ref, cf_ref, dtf_ref, xb_ref, bb_ref, cb_ref, dtb_ref,
                     bias_ref, alog_ref, yf_ref, yb_ref, stf_ref, stb_ref):
    @pl.when(pl.program_id(1) == 0)
    def _():
        stf_ref[...] = jnp.zeros_like(stf_ref)
        stb_ref[...] = jnp.zeros_like(stb_ref)

    heads = xf_ref.shape[1] // SSD_HEADDIM
    dec_f = _ssd_decays(dtf_ref, bias_ref, alog_ref, heads=heads, lane0=0, reverse=False)
    dec_b = _ssd_decays(dtb_ref, bias_ref, alog_ref, heads=heads, lane0=heads, reverse=True)
    for g in range(SSD_GROUPS):
        _ssd_group(g, xf_ref, bf_ref, cf_ref, stf_ref, yf_ref, dec_f, reverse=False)
        _ssd_group(g, xb_ref, bb_ref, cb_ref, stb_ref, yb_ref, dec_b, reverse=True)


def ssd_scan(xbc, dt_raw, dt_bias, a_log):
    bsz, seq, width = xbc.shape
    gn = SSD_GROUPS * SSD_STATE
    d_inner = width - 2 * gn
    t = SSD_CHUNK
    nc = seq // t
    xblk = d_inner // gn
    fwd = lambda b, c: (b, c)
    bwd = lambda b, c: (b, nc - 1 - c)

    def specs(pos):
        return [pl.BlockSpec((None, t, d_inner), lambda b, c: (*pos(b, c), 0)),
                pl.BlockSpec((None, t, gn), lambda b, c: (*pos(b, c), xblk)),
                pl.BlockSpec((None, t, gn), lambda b, c: (*pos(b, c), xblk + 1)),
                pl.BlockSpec((None, t, LANES), lambda b, c: (*pos(b, c), 0))]

    const = pl.BlockSpec((1, LANES), lambda b, c: (0, 0))
    state = pltpu.VMEM((SSD_GROUPS, SSD_STATE, d_inner // SSD_GROUPS), F32)
    return pl.pallas_call(
        _ssd_scan_kernel,
        grid=(bsz, nc),
        in_specs=specs(fwd) + specs(bwd) + [const, const],
        out_specs=[pl.BlockSpec((None, t, d_inner), lambda b, c: (*fwd(b, c), 0)),
                   pl.BlockSpec((None, t, d_inner), lambda b, c: (*bwd(b, c), 0))],
        out_shape=[jax.ShapeDtypeStruct((bsz, seq, d_inner), F32)] * 2,
        scratch_shapes=[state, state],
        compiler_params=_cparams(("parallel", "arbitrary")),
        name="ssd_scan",
    )(xbc, xbc, xbc, dt_raw, xbc, xbc, xbc, dt_raw, dt_bias, a_log)


def _ssd_out_kernel(yf_ref, yb_ref, xs_ref, dsk_ref, gnw_ref, w_ref, res_ref, mnw_ref, wz_ref, o_ref):
    res = res_ref[...]
    z = _dot(_rms(res, mnw_ref[...]), wz_ref[...])
    y = (yf_ref[...] + yb_ref[...] + xs_ref[...] * dsk_ref[...]) * _silu(z)
    gw = y.shape[1] // SSD_GROUPS
    y = jnp.concatenate([_rms(y[:, g * gw:(g + 1) * gw], gnw_ref[:, g * gw:(g + 1) * gw])
                         for g in range(SSD_GROUPS)], axis=1)
    o_ref[...] = res + _dot(y, w_ref[...])


def ssd_out(y_f, y_b, xbc, d_skip, gn_w, out_w, res, mix_nw, w_z):
    t, d_inner = y_f.shape
    d = out_w.shape[1]
    tm = min(ROW_TILE, t)
    row = lambda i: (i, 0)
    const = lambda i: (0, 0)
    once = pl.Buffered(1)
    return pl.pallas_call(
        _ssd_out_kernel,
        grid=(t // tm,),
        in_specs=[pl.BlockSpec((tm, d_inner), row), pl.BlockSpec((tm, d_inner), row),
                  pl.BlockSpec((tm, d_inner), row),
                  pl.BlockSpec((1, d_inner), const), pl.BlockSpec((1, d_inner), const),
                  pl.BlockSpec((d_inner, d), const, pipeline_mode=once), pl.BlockSpec((tm, d), row),
                  pl.BlockSpec((1, d), const), pl.BlockSpec((d, d_inner), const, pipeline_mode=once)],
        out_specs=pl.BlockSpec((tm, d), row),
        out_shape=jax.ShapeDtypeStruct((t, d), F32),
        compiler_params=_cparams(("parallel",)),
        name="ssd_out",
    )(y_f, y_b, xbc, d_skip, gn_w, out_w, res, mix_nw, w_z)


def _pad_rows(x, r0, total):
    parts = []
    if r0 > 0:
        parts.append(jnp.zeros((r0, x.shape[1]), x.dtype))
    parts.append(x)
    if r0 + x.shape[0] < total:
        parts.append(jnp.zeros((total - r0 - x.shape[0], x.shape[1]), x.dtype))
    return jnp.concatenate(parts, axis=0) if len(parts) > 1 else x


def _gla_chunk(q_ref, k_ref, v_ref, o_ref, bc, states, rs, *, reverse):
    ch = GLA_CHUNK
    kd = q_ref.shape[1]
    dk = kd // GLA_HEADS
    dv = v_ref.shape[1] // GLA_HEADS
    nsub = ch // GLA_SUB
    tri = _tri(ch, reverse)
    eye = lax.broadcasted_iota(jnp.int32, (dk, dk), 0) == lax.broadcasted_iota(jnp.int32, (dk, dk), 1)
    tot = bc[0:1] if reverse else bc[ch - 1:ch]
    q = q_ref[rs, :] * (dk ** -0.5)
    k = k_ref[rs, :]
    v = v_ref[rs, :].astype(MXU_DTYPE)
    q_in = q * jnp.exp(bc)
    k_st = k * jnp.exp(tot - bc)
    e_tot = jnp.exp(tot)
    new_states = []
    for h in range(GLA_HEADS):
        ks = slice(h * dk, (h + 1) * dk)
        vs = slice(h * dv, (h + 1) * dv)
        bch, qh, kh = bc[:, ks], q[:, ks], k[:, ks]
        q_segs, k_segs = [], []
        for i in range(nsub):
            l0, l1 = i * GLA_SUB, (i + 1) * GLA_SUB
            if reverse:
                s0, s1 = l0, ch
                ref = bch[l1:l1 + 1] if l1 < ch else 0.0
            else:
                s0, s1 = 0, l1
                ref = bch[l0 - 1:l0] if l0 > 0 else 0.0
            q_segs.append(_pad_rows(qh[l0:l1] * jnp.exp(bch[l0:l1] - ref), l0, ch))
            k_segs.append(_pad_rows(kh[s0:s1] * jnp.exp(ref - bch[s0:s1]), s0, ch))
        att = _dot_nt(jnp.concatenate(q_segs, axis=1), jnp.concatenate(k_segs, axis=1))
        att = jnp.where(tri, att, 0.0)
        st = states[h]
        o_ref[rs, vs] = _dot(jnp.concatenate([q_in[:, ks], att], axis=1),
                             jnp.concatenate([st.astype(MXU_DTYPE), v[:, vs]], axis=0))
        e_col = jnp.sum(jnp.where(eye, jnp.broadcast_to(e_tot[:, ks], (dk, dk)), 0.0), axis=1, keepdims=True)
        new_states.append(st * e_col + _dot_tn(k_st[:, ks], v[:, vs]))
    return new_states


def _gla_gate_sums(gl_ref, wgk_ref, bgk_ref, cols, reverse):
    x = _dot(gl_ref[...], wgk_ref[:, cols]) + bgk_ref[:, cols]
    g = -_softplus(-x) * (1.0 / GLA_GATE_NORMALIZER)
    rows = g.shape[0]
    row = lax.broadcasted_iota(jnp.int32, (rows, rows), 0)
    col = lax.broadcasted_iota(jnp.int32, (rows, rows), 1)
    same_chunk = (row // GLA_CHUNK) == (col // GLA_CHUNK)
    return _mask_times(same_chunk & ((row <= col) if reverse else (row >= col)), g)


def _gla_scan_kernel(qf_ref, kf_ref, vf_ref, glf_ref, qb_ref, kb_ref, vb_ref, glb_ref, wgk_ref, bgk_ref,
                     of_ref, ob_ref, stf_ref, stb_ref):
    @pl.when(pl.program_id(1) == 0)
    def _():
        stf_ref[...] = jnp.zeros_like(stf_ref)
        stb_ref[...] = jnp.zeros_like(stb_ref)

    ch = GLA_CHUNK
    rows, kd = qf_ref.shape
    nchunks = rows // ch
    g_f = _gla_gate_sums(glf_ref, wgk_ref, bgk_ref, slice(0, kd), reverse=False)
    g_b = _gla_gate_sums(glb_ref, wgk_ref, bgk_ref, slice(kd, 2 * kd), reverse=True)
    st_f = [stf_ref[h] for h in range(GLA_HEADS)]
    st_b = [stb_ref[h] for h in range(GLA_HEADS)]
    for i in range(nchunks):
        rf = slice(i * ch, (i + 1) * ch)
        rb = slice((nchunks - 1 - i) * ch, (nchunks - i) * ch)
        st_f = _gla_chunk(qf_ref, kf_ref, vf_ref, of_ref, g_f[rf], st_f, rf, reverse=False)
        st_b = _gla_chunk(qb_ref, kb_ref, vb_ref, ob_ref, g_b[rb], st_b, rb, reverse=True)
    for h in range(GLA_HEADS):
        stf_ref[h] = st_f[h]
        stb_ref[h] = st_b[h]


def gla_scan(qkvg, gl, wgk, bgk, key_dim, value_dim):
    bsz, seq, _ = qkvg.shape
    ch = min(GLA_BLOCK, seq)
    nc = seq // ch
    fwd = lambda b, c: (b, c)
    bwd = lambda b, c: (b, nc - 1 - c)

    def specs(pos):
        return [pl.BlockSpec((None, ch, key_dim), lambda b, c: (*pos(b, c), 0)),
                pl.BlockSpec((None, ch, key_dim), lambda b, c: (*pos(b, c), 1)),
                pl.BlockSpec((None, ch, value_dim), lambda b, c: (*pos(b, c), (2 * key_dim) // value_dim)),
                pl.BlockSpec((None, ch, LANES), lambda b, c: (*pos(b, c), 0))]

    state = pltpu.VMEM((GLA_HEADS, key_dim // GLA_HEADS, value_dim // GLA_HEADS), F32)
    return pl.pallas_call(
        _gla_scan_kernel,
        grid=(bsz, nc),
        in_specs=specs(fwd) + specs(bwd) + [pl.BlockSpec((LANES, 2 * key_dim), lambda b, c: (0, 0)),
                                            pl.BlockSpec((1, 2 * key_dim), lambda b, c: (0, 0))],
        out_specs=[pl.BlockSpec((None, ch, value_dim), lambda b, c: (*fwd(b, c), 0)),
                   pl.BlockSpec((None, ch, value_dim), lambda b, c: (*bwd(b, c), 0))],
        out_shape=[jax.ShapeDtypeStruct((bsz, seq, value_dim), F32)] * 2,
        scratch_shapes=[state, state],
        compiler_params=_cparams(("parallel", "arbitrary")),
        name="gla_scan",
    )(qkvg, qkvg, qkvg, gl, qkvg, qkvg, qkvg, gl, wgk, bgk)


def _gla_out_kernel(of_ref, ob_ref, hnw_ref, w_ref, res_ref, mnw_ref, wg_ref, o_ref):
    res = res_ref[...]
    g = _dot(_rms(res, mnw_ref[...]), wg_ref[...])
    o = of_ref[...] + ob_ref[...]
    dv = o.shape[1] // GLA_HEADS
    o = jnp.concatenate([_rms(o[:, h * dv:(h + 1) * dv], hnw_ref[...]) for h in range(GLA_HEADS)], axis=1)
    o_ref[...] = res + _dot(o * _silu(g), w_ref[...])


def gla_out(o_f, o_b, head_nw, out_w, res, mix_nw, w_g):
    t, vd = o_f.shape
    d = out_w.shape[1]
    tm = min(ROW_TILE, t)
    row = lambda i: (i, 0)
    const = lambda i: (0, 0)
    return pl.pallas_call(
        _gla_out_kernel,
        grid=(t // tm,),
        in_specs=[pl.BlockSpec((tm, vd), row), pl.BlockSpec((tm, vd), row),
                  pl.BlockSpec((1, vd // GLA_HEADS), const),
                  pl.BlockSpec((vd, d), const), pl.BlockSpec((tm, d), row),
                  pl.BlockSpec((1, d), const), pl.BlockSpec((d, vd), const)],
        out_specs=pl.BlockSpec((tm, d), row),
        out_shape=jax.ShapeDtypeStruct((t, d), F32),
        compiler_params=_cparams(("parallel",)),
        name="gla_out",
    )(o_f, o_b, head_nw, out_w, res, mix_nw, w_g)


def _hy_filter_kernel(w1_ref, b1_ref, w2_ref, b2_ref, w3_ref, freq_ref, delta_ref, o_ref, *, seq):
    tr = o_ref.shape[0]
    r = pl.program_id(1) * tr + lax.broadcasted_iota(jnp.int32, (tr, 1), 0)
    pos = jnp.where(r < seq, r, 2 * seq - r).astype(F32)
    t = pos / (seq - 1.0)
    w = (2.0 * math.pi) * pos / seq
    band = lax.broadcasted_iota(jnp.int32, (1, HY_BANDS), 1).astype(F32)
    bands = 1e-4 + band * ((HY_BANDS - 1 - 1e-4) / (HY_BANDS - 1))
    zw1 = (t * w1_ref[0:1, :] + _dot_exact(jnp.cos(bands * w), w1_ref[1:1 + HY_BANDS, :])
           + _dot_exact(-jnp.sin(bands * w), w1_ref[1 + HY_BANDS:1 + 2 * HY_BANDS, :]))
    freq = freq_ref[...]
    hid = jnp.sin(freq * (zw1 + b1_ref[...]))
    hid = jnp.sin(freq * (_dot_exact(hid, w2_ref[...]) + b2_ref[...]))
    filt = _dot_exact(hid, w3_ref[...]) * jnp.exp(-t * delta_ref[...])
    o_ref[...] = jnp.where(r == seq, 0.0, filt)


def hy_filter(seq, w1, b1, w2, b2, w3, freq, width, tr=512):
    n = 2 * seq
    tr = min(tr, seq)
    tc = width
    emb, hidden = w1.shape
    min_decay = math.log(HY_DECAY_TARGET) / HY_SLOW_DECAY_PCT
    max_decay = math.log(HY_DECAY_TARGET) / HY_FAST_DECAY_PCT
    deltas = jnp.abs(jnp.linspace(min_decay, max_decay, width, dtype=F32))[None, :]
    nct = width // tc
    half = seq // tr
    const = lambda j, i: (0, 0)
    return pl.pallas_call(
        functools.partial(_hy_filter_kernel, seq=seq),
        grid=(nct, n // tr),
        in_specs=[pl.BlockSpec((emb, hidden), const), pl.BlockSpec((1, hidden), const),
                  pl.BlockSpec((hidden, hidden), const), pl.BlockSpec((1, hidden), const),
                  pl.BlockSpec((hidden, tc), lambda j, i: (0, j + nct * (i // half))),
                  pl.BlockSpec((1, hidden), const),
                  pl.BlockSpec((1, tc), lambda j, i: (0, j))],
        out_specs=pl.BlockSpec((tr, tc), lambda j, i: (i, j)),
        out_shape=jax.ShapeDtypeStruct((n, width), F32),
        compiler_params=_cparams(("parallel", "parallel")),
        name="hy_filter",
    )(w1, b1, w2, b2, w3, freq, deltas)


def _stack_complex(m):
    return np.concatenate([np.concatenate([m.real, -m.imag], axis=-1),
                           np.concatenate([m.imag, m.real], axis=-1)], axis=-2)


@functools.lru_cache(maxsize=None)
def _dft_tables(n, n1):
    n2 = n // n1
    h1 = n1 // 2
    ang = -2.0 * np.pi / n
    k1 = np.arange(n1)[None, :, None]
    nn1 = np.arange(n1)[None, None, :]
    nn2 = np.arange(n2)[:, None, None]
    g1_full = np.exp(1j * ang * k1 * (n2 * nn1 + nn2))
    g1 = _stack_complex(g1_full[:, :, :h1])
    g1_real = np.concatenate([g1_full.real, g1_full.imag], axis=-2)
    kk2 = np.arange(n2)[:, None]
    f2 = np.exp(-2j * np.pi * kk2 * np.arange(n2)[None, :] / n2)
    f2s = _stack_complex(f2)
    f3s = _stack_complex(np.conj(f2))
    out1 = np.arange(h1)[None, :, None]
    g4 = np.exp(-1j * ang * (n2 * out1 + nn2) * np.arange(n1)[None, None, :]) / n
    g4s = _stack_complex(g4)
    cast = lambda a: jnp.asarray(a, dtype=F32).astype(MXU_DTYPE)
    return cast(g1), cast(g1_real), cast(f2s), cast(f3s), cast(g4s)


FFT_UNROLL = 8
FFT_SLABS = 4
FFT_PAD = 8


def _load_complex_strided(ref, start, rows, stride):
    return jnp.concatenate([ref[0, pl.ds(start, rows, stride=stride), :],
                            ref[1, pl.ds(start, rows, stride=stride), :]], axis=0)


def _store_complex_slab(ref, base, rows, out):
    ref[0, pl.ds(base, rows), :] = out[:rows]
    ref[1, pl.ds(base, rows), :] = out[rows:]


def _shared_matrix_stage(load, store, mat_ref, count):
    def body(jj, carry):
        j0 = jj * FFT_SLABS
        x = jnp.concatenate([load(j0 + u) for u in range(FFT_SLABS)], axis=1).astype(MXU_DTYPE)
        out = jnp.dot(mat_ref[...], x, preferred_element_type=F32)
        for u in range(FFT_SLABS):
            store(j0 + u, out[:, u * LANES:(u + 1) * LANES])
        return carry

    lax.fori_loop(0, count // FFT_SLABS, body, 0, unroll=2)


def _per_slab_matrix_stage(load, store, mats_ref, count):
    def body(j, carry):
        store(j, jnp.dot(mats_ref[j], load(j).astype(MXU_DTYPE), preferred_element_type=F32))
        return carry

    lax.fori_loop(0, count, body, 0, unroll=FFT_UNROLL)


def _slab_base(j, rows):
    return pl.multiple_of(j * (rows + FFT_PAD), 8)


def _scratch_rows(n, n1):
    n2 = n // n1
    return max(n2 * (n1 + FFT_PAD), n1 * (n2 + FFT_PAD))


def _hy_spectrum_kernel(filt_ref, g1r_ref, f2s_ref, hf_ref, a_ref, *, n1):
    n = filt_ref.shape[0]
    n2 = n // n1
    _per_slab_matrix_stage(
        lambda j: filt_ref[pl.ds(j, n1, stride=n2), :],
        lambda j, out: _store_complex_slab(a_ref, _slab_base(j, n1), n1, out),
        g1r_ref, n2)
    _shared_matrix_stage(
        lambda j: _load_complex_strided(a_ref, j, n2, n1 + FFT_PAD),
        lambda j, out: _store_complex_slab(hf_ref, pl.multiple_of(j * n2, n2), n2, out),
        f2s_ref, n1)


def hy_spectrum(filt):
    n, c = filt.shape
    n1 = FFT_N1
    _, g1r, f2s, _, _ = _dft_tables(n, n1)
    return pl.pallas_call(
        functools.partial(_hy_spectrum_kernel, n1=n1),
        grid=(c // LANES,),
        in_specs=[pl.BlockSpec((n, LANES), lambda j: (0, j)),
                  pl.BlockSpec(g1r.shape, lambda j: (0, 0, 0)),
                  pl.BlockSpec(f2s.shape, lambda j: (0, 0))],
        out_specs=pl.BlockSpec((2, n, LANES), lambda j: (0, 0, j)),
        out_shape=jax.ShapeDtypeStruct((2, n, c), F32),
        scratch_shapes=[pltpu.VMEM((2, _scratch_rows(n, n1), LANES), F32)],
        compiler_params=_cparams(("parallel",)),
        name="hy_spectrum",
    )(filt, g1r, f2s)


def _hy_fftconv_kernel(v_ref, hf_ref, g1_ref, f2s_ref, f3s_ref, g4s_ref, o_ref, a_ref, y_ref, *, n1):
    seq = v_ref.shape[1]
    n = 2 * seq
    n2 = n // n1
    h1 = n1 // 2

    def multiply_by_spectrum(j, out):
        base = pl.multiple_of(j * n2, n2)
        xr, xi = out[:n2], out[n2:]
        hr = hf_ref[0, pl.ds(base, n2), :]
        hi = hf_ref[1, pl.ds(base, n2), :]
        y_ref[0, pl.ds(base, n2), :] = xr * hr - xi * hi
        y_ref[1, pl.ds(base, n2), :] = xr * hi + xi * hr

    def stage_input(m, carry):
        for comp in range(2):
            y_ref[comp, pl.ds(_slab_base(m, n2), n2), :] = v_ref[comp, pl.ds(pl.multiple_of(m * n2, n2), n2), :]
        return carry

    lax.fori_loop(0, h1, stage_input, 0)

    def stage_output(j, out):
        _store_complex_slab(y_ref, _slab_base(j, h1), h1, out)

    def unstage_output(m, carry):
        for comp in range(2):
            o_ref[comp, pl.ds(pl.multiple_of(m * n2, n2), n2), :] = y_ref[comp, pl.ds(m, n2, stride=h1 + FFT_PAD), :]
        return carry

    _per_slab_matrix_stage(
        lambda j: _load_complex_strided(y_ref, j, h1, n2 + FFT_PAD),
        lambda j, out: _store_complex_slab(a_ref, _slab_base(j, n1), n1, out),
        g1_ref, n2)
    _shared_matrix_stage(lambda j: _load_complex_strided(a_ref, j, n2, n1 + FFT_PAD), multiply_by_spectrum,
                         f2s_ref, n1)
    _shared_matrix_stage(
        lambda j: jnp.concatenate([y_ref[0, pl.ds(pl.multiple_of(j * n2, n2), n2), :],
                                   y_ref[1, pl.ds(pl.multiple_of(j * n2, n2), n2), :]], axis=0),
        lambda j, out: _store_complex_slab(a_ref, _slab_base(j, n2), n2, out),
        f3s_ref, n1)
    _per_slab_matrix_stage(lambda j: _load_complex_strided(a_ref, j, n1, n2 + FFT_PAD), stage_output, g4s_ref, n2)
    lax.fori_loop(0, h1, unstage_output, 0)


def hy_fftconv(vx, hf):
    bsz, seq, c = vx.shape
    n = 2 * seq
    n1 = FFT_N1
    g1, _, f2s, f3s, g4s = _dft_tables(n, n1)
    once = pl.Buffered(1)
    tab3 = lambda a: pl.BlockSpec(a.shape, lambda j, p: (0, 0, 0), pipeline_mode=once)
    tab2 = lambda a: pl.BlockSpec(a.shape, lambda j, p: (0, 0), pipeline_mode=once)
    pair = pl.BlockSpec((None, 2, seq, LANES), lambda j, p: (p, 0, 0, j))
    out = pl.pallas_call(
        functools.partial(_hy_fftconv_kernel, n1=n1),
        grid=(c // LANES, bsz // 2),
        in_specs=[pair,
                  pl.BlockSpec((2, n, LANES), lambda j, p: (0, 0, j), pipeline_mode=once),
                  tab3(g1), tab2(f2s), tab2(f3s), tab3(g4s)],
        out_specs=pair,
        out_shape=jax.ShapeDtypeStruct((bsz // 2, 2, seq, c), F32),
        scratch_shapes=[pltpu.VMEM((2, _scratch_rows(n, n1), LANES), F32), pltpu.VMEM((2, n, LANES), F32)],
        compiler_params=_cparams(("parallel", "parallel")),
        name="hy_fftconv",
    )(vx.reshape(bsz // 2, 2, seq, c), hf, g1, f2s, f3s, g4s)
    return out.reshape(bsz, seq, c)


def _hy_out_kernel(conv_ref, vx_ref, x0_ref, skip_ref, w_ref, b_ref, res_ref, o_ref):
    y = (conv_ref[...] + vx_ref[...] * skip_ref[...]) * x0_ref[...]
    o_ref[...] = res_ref[...] + _dot(y, w_ref[...]) + b_ref[...]


def hy_out(conv, vx, x0c, skip, out_w, out_b, res):
    t, c = vx.shape
    d = out_w.shape[1]
    tm = min(ROW_TILE, t)
    row = lambda i: (i, 0)
    const = lambda i: (0, 0)
    return pl.pallas_call(
        _hy_out_kernel,
        grid=(t // tm,),
        in_specs=[pl.BlockSpec((tm, c), row), pl.BlockSpec((tm, c), row), pl.BlockSpec((tm, c), row),
                  pl.BlockSpec((1, c), const), pl.BlockSpec((c, d), const), pl.BlockSpec((1, d), const),
                  pl.BlockSpec((tm, d), row)],
        out_specs=pl.BlockSpec((tm, d), row),
        out_shape=jax.ShapeDtypeStruct((t, d), F32),
        compiler_params=_cparams(("parallel",)),
        name="hy_out",
    )(conv, vx, x0c, skip, out_w, out_b, res)


def _pad_cols(a, width):
    return jnp.pad(a, ((0, 0), (0, width - a.shape[1])))


def _col_tile(n):
    for tile in (1024, 768, 512, 256, 128):
        if n % tile == 0:
            return tile
    raise ValueError(f"unsupported projection width {n}")


def mamba2_mixer(h, bsz, norm_w, in_w, conv_w, conv_b, dt_bias, a_log, d_skip, gn_w, out_w):
    t, d = h.shape
    seq = t // bsz
    heads = a_log.shape[1]
    d_inner = heads * SSD_HEADDIM
    conv_dim = conv_w.shape[1]
    main = d_inner + conv_dim
    w_dt = _pad_cols(in_w[:, main:], LANES).astype(MXU_DTYPE)
    xbc, dt_raw = norm_linear_conv(h, seq, norm_w, in_w[:, d_inner:main].astype(MXU_DTYPE),
                                   jnp.zeros((1, conv_dim), F32), conv_w, conv_b[None, :], _col_tile(conv_dim),
                                   _silu_epilogue, (_col_tile(conv_dim),), w_side=w_dt)
    y_f, y_b = ssd_scan(xbc.reshape(bsz, seq, conv_dim), dt_raw.reshape(bsz, seq, LANES),
                        _pad_cols(dt_bias.reshape(1, 2 * heads), LANES),
                        _pad_cols(a_log.reshape(1, 2 * heads), LANES))
    return ssd_out(y_f.reshape(t, d_inner), y_b.reshape(t, d_inner), xbc, jnp.repeat(d_skip, SSD_HEADDIM)[None, :],
                   gn_w[None, :], out_w.astype(MXU_DTYPE), h, norm_w, in_w[:, :d_inner].astype(MXU_DTYPE))


def hyena_mixer(h, bsz, norm_w, in_w, in_b, conv_w, conv_b, f_w1, f_b1, f_w2, f_b2, f_w3, sin_freq, skip,
                out_w, out_b):
    t, d = h.shape
    seq = t // bsz
    width = skip.shape[0]
    sub = HY_PART_TILE
    by_tile = lambda a: a.reshape(a.shape[0], 3, width // sub, sub).transpose(0, 2, 1, 3).reshape(a.shape[0], 3 * width)
    x0c, vx = norm_linear_conv(h, seq, norm_w, by_tile(in_w).astype(MXU_DTYPE), by_tile(in_b[None, :]),
                               by_tile(conv_w), by_tile(conv_b[None, :]), 3 * sub, _hyena_epilogue, (sub, sub))
    filt = hy_filter(seq, f_w1, f_b1[None, :], f_w2, f_b2[None, :], f_w3, sin_freq[None, :], width)
    conv = hy_fftconv(vx.reshape(bsz, seq, width), hy_spectrum(filt))
    return hy_out(conv.reshape(t, width), vx, x0c, skip[None, :], out_w.astype(MXU_DTYPE), out_b[None, :], h)


def gla_mixer(h, bsz, norm_w, in_w, gk_w, gk_b, hn_w, out_w):
    t, d = h.shape
    seq = t // bsz
    rank, key_dim = gk_w.shape[1], gk_w.shape[2]
    value_dim = out_w.shape[0]
    qkv_dim = 2 * key_dim + value_dim
    main = qkv_dim + value_dim
    w_gl = _pad_cols(in_w[:, main:], LANES).astype(MXU_DTYPE)
    qkv, gl = norm_linear(h, norm_w, in_w[:, :qkv_dim].astype(MXU_DTYPE), w_gl, _col_tile(qkv_dim))
    wgk = jnp.zeros((LANES, 2 * key_dim), F32)
    wgk = wgk.at[:rank, :key_dim].set(gk_w[0]).at[rank:2 * rank, key_dim:].set(gk_w[1]).astype(MXU_DTYPE)
    o_f, o_b = gla_scan(qkv.reshape(bsz, seq, qkv_dim), gl.reshape(bsz, seq, LANES), wgk,
                        gk_b.reshape(1, 2 * key_dim), key_dim, value_dim)
    return gla_out(o_f.reshape(t, value_dim), o_b.reshape(t, value_dim), hn_w[None, :], out_w.astype(MXU_DTYPE), h,
                   norm_w, in_w[:, qkv_dim:main].astype(MXU_DTYPE))


def kernel(x, p, norm_mix, norm_ffn, norm_ple, ple_gate, ple_proj, ffn_w1, ffn_w3, ffn_w2, final_norm,
           ssd_in_w, ssd_conv_w, ssd_conv_b, ssd_dt_bias, ssd_a_log, ssd_d, ssd_norm, ssd_out_w,
           hy_in_w, hy_in_b, hy_conv_w, hy_conv_b, hy_f_w1, hy_f_b1, hy_f_w2, hy_f_b2, hy_f_w3,
           hy_sin_freq, hy_skip, hy_out_w, hy_out_b,
           gla_in_w, gla_gk_w, gla_gk_b, gla_norm, gla_out_w):
    bsz, seq, d = x.shape
    depth = p.shape[0]
    t = bsz * seq
    h = x.reshape(t, d)
    for i in range(depth):
        kind, j = i % N_MIXERS, i // N_MIXERS
        nw = norm_mix[i][None, :]
        if kind == 0:
            h = mamba2_mixer(h, bsz, nw, ssd_in_w[j], ssd_conv_w[j], ssd_conv_b[j], ssd_dt_bias[j], ssd_a_log[j],
                             ssd_d[j], ssd_norm[j], ssd_out_w[j])
        elif kind == 1:
            h = hyena_mixer(h, bsz, nw, hy_in_w[j], hy_in_b[j], hy_conv_w[j], hy_conv_b[j], hy_f_w1[j], hy_f_b1[j],
                            hy_f_w2[j], hy_f_b2[j], hy_f_w3[j], hy_sin_freq[j], hy_skip[j], hy_out_w[j], hy_out_b[j])
        else:
            h = gla_mixer(h, bsz, nw, gla_in_w[j], gla_gk_w[j], gla_gk_b[j], gla_norm[j], gla_out_w[j])
        h = ffn_ple(h, norm_ffn[i][None, :], ffn_w1[i].astype(MXU_DTYPE), ffn_w3[i].astype(MXU_DTYPE),
                    ffn_w2[i].astype(MXU_DTYPE), p[i].reshape(t, -1), norm_ple[i][None, :],
                    ple_gate[i].astype(MXU_DTYPE), ple_proj[i].astype(MXU_DTYPE), final_norm[None, :],
                    final=(i == depth - 1))
    return h.reshape(bsz, seq, d)
```

```python
import functools
import math

import numpy as np
import jax
import jax.numpy as jnp
from jax import lax
from jax.experimental import pallas as pl
from jax.experimental.pallas import tpu as pltpu

F32 = jnp.float32
MXU_DTYPE = jnp.bfloat16
EXACT = lax.Precision.HIGHEST

NORM_EPS = 1e-6
N_MIXERS = 3

VMEM_LIMIT_BYTES = 56 * 1024 * 1024
ROW_TILE = 512
PROJ_ROW_TILE = 1024
LANES = 128
FFN_SLICE = 512

SSD_HEADDIM = 64
SSD_GROUPS = 8
SSD_STATE = 128
SSD_CHUNK = 128
SSD_PAIR = 2 * SSD_HEADDIM
assert SSD_PAIR == LANES == SSD_CHUNK
GLA_HEADS = 4
GLA_CHUNK = 64
GLA_BLOCK = 256
GLA_SUB = 16
GLA_GATE_RANK = 16
GLA_GATE_NORMALIZER = 16.0
HY_PART_TILE = 256
HY_BANDS = 16
HY_DECAY_TARGET = 1e-2
HY_FAST_DECAY_PCT = 0.3
HY_SLOW_DECAY_PCT = 1.5
FFT_N1 = 64


def _cparams(sem, flags=None):
    return pltpu.CompilerParams(dimension_semantics=sem, vmem_limit_bytes=VMEM_LIMIT_BYTES, flags=flags)


def _dot(a, b):
    return jnp.dot(a.astype(MXU_DTYPE), b.astype(MXU_DTYPE), preferred_element_type=F32)


def _dot_nt(a, b):
    return lax.dot_general(a.astype(MXU_DTYPE), b.astype(MXU_DTYPE), (((1,), (1,)), ((), ())),
                           preferred_element_type=F32)


def _dot_tn(a, b):
    return lax.dot_general(a.astype(MXU_DTYPE), b.astype(MXU_DTYPE), (((0,), (0,)), ((), ())),
                           preferred_element_type=F32)


def _dot_exact(a, b):
    return jnp.dot(a, b, preferred_element_type=F32, precision=EXACT)


def _split3(x):
    hi = x.astype(MXU_DTYPE)
    rest = x - hi.astype(F32)
    mid = rest.astype(MXU_DTYPE)
    lo = (rest - mid.astype(F32)).astype(MXU_DTYPE)
    return hi, mid, lo


def _mask_times(mask, x):
    m = jnp.where(mask, 1.0, 0.0).astype(MXU_DTYPE)
    return jnp.dot(jnp.concatenate([m, m, m], axis=1), jnp.concatenate(_split3(x), axis=0),
                   preferred_element_type=F32)


def _times_mask(x, mask):
    m = jnp.where(mask, 1.0, 0.0).astype(MXU_DTYPE)
    return jnp.dot(jnp.concatenate(_split3(x), axis=1), jnp.concatenate([m, m, m], axis=0),
                   preferred_element_type=F32)


def _rms(x, w):
    return x * lax.rsqrt(jnp.mean(x * x, axis=-1, keepdims=True) + NORM_EPS) * w


def _sigmoid(x):
    return 1.0 / (1.0 + jnp.exp(-x))


def _silu(x):
    return x * _sigmoid(x)


def _softplus(x):
    return jnp.maximum(x, 0.0) + jnp.log(1.0 + jnp.exp(-jnp.abs(x)))


def _norm_linear_kernel(x_ref, nw_ref, w_ref, ws_ref, o_ref, os_ref, xn_ref):
    @pl.when(pl.program_id(1) == 0)
    def _():
        xn_ref[...] = _rms(x_ref[...], nw_ref[...]).astype(xn_ref.dtype)
        os_ref[...] = jnp.dot(xn_ref[...], ws_ref[...], preferred_element_type=F32)

    o_ref[...] = jnp.dot(xn_ref[...], w_ref[...], preferred_element_type=F32)


def norm_linear(x, nw, w, w_side, col_tile):
    t, d = x.shape
    n = w.shape[1]
    tm = min(PROJ_ROW_TILE, t)
    return pl.pallas_call(
        _norm_linear_kernel,
        grid=(t // tm, n // col_tile),
        in_specs=[pl.BlockSpec((tm, d), lambda i, j: (i, 0)),
                  pl.BlockSpec((1, d), lambda i, j: (0, 0)),
                  pl.BlockSpec((d, col_tile), lambda i, j: (0, j)),
                  pl.BlockSpec((d, LANES), lambda i, j: (0, 0))],
        out_specs=[pl.BlockSpec((tm, col_tile), lambda i, j: (i, j)),
                   pl.BlockSpec((tm, LANES), lambda i, j: (i, 0))],
        out_shape=[jax.ShapeDtypeStruct((t, n), F32), jax.ShapeDtypeStruct((t, LANES), F32)],
        scratch_shapes=[pltpu.VMEM((tm, d), MXU_DTYPE)],
        compiler_params=_cparams(("parallel", "arbitrary")),
        name="norm_linear",
    )(x, nw, w, w_side)


def _ffn_ple_kernel(h_ref, nf_ref, w1_ref, w3_ref, w2_ref, p_ref, npl_ref, g_ref, pp_ref, fn_ref,
                    o_ref, xn_ref, acc_ref, *, final):
    f = pl.program_id(1)

    @pl.when(f == 0)
    def _():
        xn_ref[...] = _rms(h_ref[...], nf_ref[...]).astype(xn_ref.dtype)
        acc_ref[...] = jnp.zeros_like(acc_ref)

    xn = xn_ref[...]
    tf = w1_ref.shape[1]
    total = None
    for c0 in range(0, tf, FFN_SLICE):
        cs = slice(c0, min(c0 + FFN_SLICE, tf))
        a = jnp.dot(xn, w1_ref[:, cs], preferred_element_type=F32)
        b = jnp.dot(xn, w3_ref[:, cs], preferred_element_type=F32)
        part = _dot(_silu(a) * b, w2_ref[cs, :])
        total = part if total is None else total + part
    acc_ref[...] += total

    @pl.when(f == pl.num_programs(1) - 1)
    def _():
        h2 = h_ref[...] + acc_ref[...]
        gate = _sigmoid(_dot(_rms(h2, npl_ref[...]), g_ref[...]))
        h3 = h2 + gate * _dot(p_ref[...], pp_ref[...])
        if final:
            h3 = _rms(h3, fn_ref[...])
        o_ref[...] = h3


def ffn_ple(h, nf, w1, w3, w2, p, npl, g, pp, fn, final):
    t, d = h.shape
    dff = w1.shape[1]
    tf = dff // 2
    pdim = p.shape[1]
    tm = min(ROW_TILE, t)
    row = lambda i, f: (i, 0)
    const = lambda i, f: (0, 0)
    return pl.pallas_call(
        functools.partial(_ffn_ple_kernel, final=final),
        grid=(t // tm, dff // tf),
        in_specs=[pl.BlockSpec((tm, d), row),
                  pl.BlockSpec((1, d), const),
                  pl.BlockSpec((d, tf), lambda i, f: (0, f)),
                  pl.BlockSpec((d, tf), lambda i, f: (0, f)),
                  pl.BlockSpec((tf, d), lambda i, f: (f, 0)),
                  pl.BlockSpec((tm, pdim), row),
                  pl.BlockSpec((1, d), const),
                  pl.BlockSpec((d, d), const),
                  pl.BlockSpec((pdim, d), const),
                  pl.BlockSpec((1, d), const)],
        out_specs=pl.BlockSpec((tm, d), row),
        out_shape=jax.ShapeDtypeStruct((t, d), F32),
        scratch_shapes=[pltpu.VMEM((tm, d), MXU_DTYPE), pltpu.VMEM((tm, d), F32)],
        compiler_params=_cparams(("parallel", "arbitrary")),
        name="ffn_ple",
    )(h, nf, w1, w3, w2, p, npl, g, pp, fn)


CONV_HALO = 16
CONV_ROWS = 256


def _norm_linear_conv_kernel(xp_ref, x_ref, xn_ref, nw_ref, w_ref, b_ref, cw_ref, cb_ref, *rest,
                             taps, tiles_per_seq, epilogue, side):
    ws_ref, rest = (rest[0], rest[1:]) if side else (None, rest)
    *out_refs, xs_ref = rest
    os_ref = out_refs.pop() if side else None
    tm = x_ref.shape[0]
    i = pl.program_id(0)

    @pl.when(pl.program_id(1) == 0)
    def _():
        nw = nw_ref[...]
        xs_ref[0:CONV_HALO, :] = _rms(xp_ref[...], nw).astype(xs_ref.dtype)
        xs_ref[CONV_HALO:CONV_HALO + tm, :] = _rms(x_ref[...], nw).astype(xs_ref.dtype)
        xs_ref[CONV_HALO + tm:, :] = _rms(xn_ref[...], nw).astype(xs_ref.dtype)
        if side:
            os_ref[...] = jnp.dot(xs_ref[CONV_HALO:CONV_HALO + tm, :], ws_ref[...], preferred_element_type=F32)

    first = (i % tiles_per_seq) == 0
    last = (i % tiles_per_seq) == tiles_per_seq - 1
    rows = min(CONV_ROWS, tm)
    nchunks = tm // rows
    for r in range(nchunks):
        base = r * rows
        win = jnp.dot(xs_ref[base:base + rows + 2 * CONV_HALO, :], w_ref[...],
                      preferred_element_type=F32) + b_ref[...]
        top, bottom = win[:CONV_HALO], win[CONV_HALO + rows:]
        if r == 0:
            top = jnp.where(first, 0.0, top)
        if r == nchunks - 1:
            bottom = jnp.where(last, 0.0, bottom)
        win = jnp.concatenate([top, win[CONV_HALO:CONV_HALO + rows], bottom], axis=0)
        acc = cb_ref[...]
        for k in range(taps):
            shift = (taps // 2 - k) % win.shape[0]
            moved = pltpu.roll(win, shift, 0) if shift else win
            acc = acc + cw_ref[k:k + 1, :] * moved[CONV_HALO:CONV_HALO + rows, :]
        epilogue(out_refs, slice(base, base + rows), acc)


def _silu_epilogue(out_refs, rows, acc):
    out_refs[0][rows, :] = _silu(acc)


def _hyena_epilogue(out_refs, rows, acc):
    c = acc.shape[1] // 3
    out_refs[0][rows, :] = acc[:, :c]
    out_refs[1][rows, :] = acc[:, 2 * c:] * acc[:, c:2 * c]


def norm_linear_conv(x, seq, nw, w, b, conv_w, conv_b, col_tile, epilogue, out_widths, w_side=None):
    t, d = x.shape
    n = w.shape[1]
    taps = conv_w.shape[0]
    tm = min(PROJ_ROW_TILE, seq)
    per_tile = tm // CONV_HALO
    last_halo = t // CONV_HALO - 1
    side = w_side is not None
    in_specs = [pl.BlockSpec((CONV_HALO, d), lambda i, j: (jnp.maximum(i * per_tile - 1, 0), 0)),
                pl.BlockSpec((tm, d), lambda i, j: (i, 0)),
                pl.BlockSpec((CONV_HALO, d), lambda i, j: (jnp.minimum((i + 1) * per_tile, last_halo), 0)),
                pl.BlockSpec((1, d), lambda i, j: (0, 0)),
                pl.BlockSpec((d, col_tile), lambda i, j: (0, j)),
                pl.BlockSpec((1, col_tile), lambda i, j: (0, j)),
                pl.BlockSpec((taps, col_tile), lambda i, j: (0, j)),
                pl.BlockSpec((1, col_tile), lambda i, j: (0, j))]
    out_specs = [pl.BlockSpec((tm, ow), lambda i, j: (i, j)) for ow in out_widths]
    out_shape = [jax.ShapeDtypeStruct((t, n // col_tile * ow), F32) for ow in out_widths]
    args = [x, x, x, nw, w, b, conv_w, conv_b]
    if side:
        in_specs.append(pl.BlockSpec((d, LANES), lambda i, j: (0, 0)))
        out_specs.append(pl.BlockSpec((tm, LANES), lambda i, j: (i, 0)))
        out_shape.append(jax.ShapeDtypeStruct((t, LANES), F32))
        args.append(w_side)
    return pl.pallas_call(
        functools.partial(_norm_linear_conv_kernel, taps=taps, tiles_per_seq=seq // tm, epilogue=epilogue, side=side),
        grid=(t // tm, n // col_tile),
        in_specs=in_specs,
        out_specs=out_specs,
        out_shape=out_shape,
        scratch_shapes=[pltpu.VMEM((tm + 2 * CONV_HALO, d), MXU_DTYPE)],
        compiler_params=_cparams(("parallel", "arbitrary")),
        name="norm_linear_conv",
    )(*args)


def _tri(n, reverse):
    row = lax.broadcasted_iota(jnp.int32, (n, n), 0)
    col = lax.broadcasted_iota(jnp.int32, (n, n), 1)
    return (row <= col) if reverse else (row >= col)


def _ssd_decays(dt_ref, dtb_ref, alog_ref, *, heads, lane0, reverse):
    t = SSD_CHUNK
    dt_all = _softplus(dt_ref[...] + dtb_ref[...])
    da_all = dt_all * (-jnp.exp(alog_ref[...]))
    da = da_all[:, lane0:lane0 + heads]
    dt_t = dt_all.T[lane0:lane0 + heads, :]
    da_t = da_all.T[lane0:lane0 + heads, :]
    acum = _mask_times(_tri(t, reverse), da)
    acum_t = _times_mask(da_t, _tri(t, not reverse))
    tot_c = acum_t[:, 0:1] if reverse else acum_t[:, t - 1:t]
    w_t = jnp.exp(tot_c - acum_t) * dt_t
    src_t = acum_t - jnp.log(dt_t)
    etot = jnp.exp(jnp.sum(da, axis=0, keepdims=True))
    return acum, src_t, w_t, etot


def _ssd_group(g, xs_ref, b_ref, c_ref, st_ref, y_ref, decays, *, reverse):
    t = SSD_CHUNK
    acum, src_t, w_t, etot = decays
    per_group = xs_ref.shape[1] // SSD_HEADDIM // SSD_GROUPS
    gw = per_group * SSD_HEADDIM
    tri = _tri(t, reverse)
    first = lax.broadcasted_iota(jnp.int32, (t, SSD_PAIR), 1) < SSD_HEADDIM
    bg_t = b_ref[:, g * SSD_STATE:(g + 1) * SSD_STATE].T
    cg = c_ref[:, g * SSD_STATE:(g + 1) * SSD_STATE].astype(MXU_DTYPE)
    cb = _dot(cg, bg_t)
    st = st_ref[g]
    y_intra, e_acum, st_new = [], [], []
    for pr in range(per_group // 2):
        h0 = g * per_group + 2 * pr
        x_pair = xs_ref[:, h0 * SSD_HEADDIM:(h0 + 2) * SSD_HEADDIM]
        rhs = jnp.concatenate([jnp.where(first, x_pair, 0.0),
                               jnp.where(first, 0.0, x_pair)], axis=0).astype(MXU_DTYPE)
        cols = [jnp.broadcast_to(acum[:, h:h + 1], (t, SSD_PAIR)) for h in (h0, h0 + 1)]
        scores = [cb * jnp.exp(jnp.where(tri, cols[k] - src_t[h0 + k:h0 + k + 1, :], -jnp.inf)) for k in range(2)]
        y_intra.append(_dot(jnp.concatenate(scores, axis=1), rhs))
        e_acum.append(jnp.where(first, jnp.exp(cols[0]), jnp.exp(cols[1])))
        e_tot = jnp.where(first[0:1], jnp.broadcast_to(etot[:, h0:h0 + 1], (1, SSD_PAIR)),
                          jnp.broadcast_to(etot[:, h0 + 1:h0 + 2], (1, SSD_PAIR)))
        b_weighted = jnp.concatenate([bg_t * w_t[h0:h0 + 1, :], bg_t * w_t[h0 + 1:h0 + 2, :]], axis=1)
        st_new.append(st[:, pr * SSD_PAIR:(pr + 1) * SSD_PAIR] * e_tot + _dot(b_weighted, rhs))
    y_ref[:, g * gw:(g + 1) * gw] = (_dot(cg, st) * jnp.concatenate(e_acum, axis=1)
                                     + jnp.concatenate(y_intra, axis=1))
    st_ref[g] = jnp.concatenate(st_new, axis=1)


def _ssd_scan_kernel(xf_ref, bf_ref, cf_ref, dtf_ref, xb_ref, bb_ref, cb_ref, dtb_ref,
                     bias_ref, alog_ref, yf_ref, yb_ref, stf_ref, stb_ref):
    @pl.when(pl.program_id(1) == 0)
    def _():
        stf_ref[...] = jnp.zeros_like(stf_ref)
        stb_ref[...] = jnp.zeros_like(stb_ref)

    heads = xf_ref.shape[1] // SSD_HEADDIM
    dec_f = _ssd_decays(dtf_ref, bias_ref, alog_ref, heads=heads, lane0=0, reverse=False)
    dec_b = _ssd_decays(dtb_ref, bias_ref, alog_ref, heads=heads, lane0=heads, reverse=True)
    for g in range(SSD_GROUPS):
        _ssd_group(g, xf_ref, bf_ref, cf_ref, stf_ref, yf_ref, dec_f, reverse=False)
        _ssd_group(g, xb_ref, bb_ref, cb_ref, stb_ref, yb_ref, dec_b, reverse=True)


def ssd_scan(xbc, dt_raw, dt_bias, a_log):
    bsz, seq, width = xbc.shape
    gn = SSD_GROUPS * SSD_STATE
    d_inner = width - 2 * gn
    t = SSD_CHUNK
    nc = seq // t
    xblk = d_inner // gn
    fwd = lambda b, c: (b, c)
    bwd = lambda b, c: (b, nc - 1 - c)

    def specs(pos):
        return [pl.BlockSpec((None, t, d_inner), lambda b, c: (*pos(b, c), 0)),
                pl.BlockSpec((None, t, gn), lambda b, c: (*pos(b, c), xblk)),
                pl.BlockSpec((None, t, gn), lambda b, c: (*pos(b, c), xblk + 1)),
                pl.BlockSpec((None, t, LANES), lambda b, c: (*pos(b, c), 0))]

    const = pl.BlockSpec((1, LANES), lambda b, c: (0, 0))
    state = pltpu.VMEM((SSD_GROUPS, SSD_STATE, d_inner // SSD_GROUPS), F32)
    return pl.pallas_call(
        _ssd_scan_kernel,
        grid=(bsz, nc),
        in_specs=specs(fwd) + specs(bwd) + [const, const],
        out_specs=[pl.BlockSpec((None, t, d_inner), lambda b, c: (*fwd(b, c), 0)),
                   pl.BlockSpec((None, t, d_inner), lambda b, c: (*bwd(b, c), 0))],
        out_shape=[jax.ShapeDtypeStruct((bsz, seq, d_inner), F32)] * 2,
        scratch_shapes=[state, state],
        compiler_params=_cparams(("parallel", "arbitrary")),
        name="ssd_scan",
    )(xbc, xbc, xbc, dt_raw, xbc, xbc, xbc, dt_raw, dt_bias, a_log)


def _ssd_out_kernel(yf_ref, yb_ref, xs_ref, dsk_ref, gnw_ref, w_ref, res_ref, mnw_ref, wz_ref, o_ref):
    res = res_ref[...]
    z = _dot(_rms(res, mnw_ref[...]), wz_ref[...])
    y = (yf_ref[...] + yb_ref[...] + xs_ref[...] * dsk_ref[...]) * _silu(z)
    gw = y.shape[1] // SSD_GROUPS
    y = jnp.concatenate([_rms(y[:, g * gw:(g + 1) * gw], gnw_ref[:, g * gw:(g + 1) * gw])
                         for g in range(SSD_GROUPS)], axis=1)
    o_ref[...] = res + _dot(y, w_ref[...])


def ssd_out(y_f, y_b, xbc, d_skip, gn_w, out_w, res, mix_nw, w_z):
    t, d_inner = y_f.shape
    d = out_w.shape[1]
    tm = min(ROW_TILE, t)
    row = lambda i: (i, 0)
    const = lambda i: (0, 0)
    once = pl.Buffered(1)
    return pl.pallas_call(
        _ssd_out_kernel,
        grid=(t // tm,),
        in_specs=[pl.BlockSpec((tm, d_inner), row), pl.BlockSpec((tm, d_inner), row),
                  pl.BlockSpec((tm, d_inner), row),
                  pl.BlockSpec((1, d_inner), const), pl.BlockSpec((1, d_inner), const),
                  pl.BlockSpec((d_inner, d), const, pipeline_mode=once), pl.BlockSpec((tm, d), row),
                  pl.BlockSpec((1, d), const), pl.BlockSpec((d, d_inner), const, pipeline_mode=once)],
        out_specs=pl.BlockSpec((tm, d), row),
        out_shape=jax.ShapeDtypeStruct((t, d), F32),
        compiler_params=_cparams(("parallel",)),
        name="ssd_out",
    )(y_f, y_b, xbc, d_skip, gn_w, out_w, res, mix_nw, w_z)


def _pad_rows(x, r0, total):
    parts = []
    if r0 > 0:
        parts.append(jnp.zeros((r0, x.shape[1]), x.dtype))
    parts.append(x)
    if r0 + x.shape[0] < total:
        parts.append(jnp.zeros((total - r0 - x.shape[0], x.shape[1]), x.dtype))
    return jnp.concatenate(parts, axis=0) if len(parts) > 1 else x


def _gla_chunk(q_ref, k_ref, v_ref, o_ref, bc, states, rs, att_ref, slot, *, reverse):
    ch = GLA_CHUNK
    kd = q_ref.shape[1]
    dk = kd // GLA_HEADS
    dv = v_ref.shape[1] // GLA_HEADS
    eye = lax.broadcasted_iota(jnp.int32, (dk, dk), 0) == lax.broadcasted_iota(jnp.int32, (dk, dk), 1)
    tot = bc[0:1] if reverse else bc[ch - 1:ch]
    v = v_ref[rs, :].astype(MXU_DTYPE)
    q_in = q_ref[rs, :] * (dk ** -0.5) * jnp.exp(bc)
    k_st = k_ref[rs, :] * jnp.exp(tot - bc)
    e_tot = jnp.exp(tot)
    new_states = []
    for h in range(GLA_HEADS):
        ks = slice(h * dk, (h + 1) * dk)
        vs = slice(h * dv, (h + 1) * dv)
        st = states[h]
        o_ref[rs, vs] = _dot(jnp.concatenate([q_in[:, ks], att_ref[slot * GLA_HEADS + h]], axis=1),
                             jnp.concatenate([st.astype(MXU_DTYPE), v[:, vs]], axis=0))
        e_col = jnp.sum(jnp.where(eye, jnp.broadcast_to(e_tot[:, ks], (dk, dk)), 0.0), axis=1, keepdims=True)
        new_states.append(st * e_col + _dot_tn(k_st[:, ks], v[:, vs]))
    return new_states


def _gla_scores(q_ref, k_ref, bc, rs, att_ref, slot, *, reverse):
    ch = GLA_CHUNK
    dk = q_ref.shape[1] // GLA_HEADS
    nsub = ch // GLA_SUB
    tri = _tri(ch, reverse)
    q = q_ref[rs, :] * (dk ** -0.5)
    k = k_ref[rs, :]
    for h in range(GLA_HEADS):
        ks = slice(h * dk, (h + 1) * dk)
        bch, qh, kh = bc[:, ks], q[:, ks], k[:, ks]
        q_segs, k_segs = [], []
        for i in range(nsub):
            l0, l1 = i * GLA_SUB, (i + 1) * GLA_SUB
            if reverse:
                s0, s1 = l0, ch
                ref = bch[l1:l1 + 1] if l1 < ch else 0.0
            else:
                s0, s1 = 0, l1
                ref = bch[l0 - 1:l0] if l0 > 0 else 0.0
            q_segs.append(_pad_rows(qh[l0:l1] * jnp.exp(bch[l0:l1] - ref), l0, ch))
            k_segs.append(_pad_rows(kh[s0:s1] * jnp.exp(ref - bch[s0:s1]), s0, ch))
        att = _dot_nt(jnp.concatenate(q_segs, axis=1), jnp.concatenate(k_segs, axis=1))
        att_ref[slot * GLA_HEADS + h] = jnp.where(tri, att, 0.0).astype(att_ref.dtype)


def _gla_gate_sums(gl_ref, wgk_ref, bgk_ref, cols, reverse):
    x = _dot(gl_ref[...], wgk_ref[:, cols]) + bgk_ref[:, cols]
    g = -_softplus(-x) * (1.0 / GLA_GATE_NORMALIZER)
    rows = g.shape[0]
    row = lax.broadcasted_iota(jnp.int32, (rows, rows), 0)
    col = lax.broadcasted_iota(jnp.int32, (rows, rows), 1)
    same_chunk = (row // GLA_CHUNK) == (col // GLA_CHUNK)
    return _mask_times(same_chunk & ((row <= col) if reverse else (row >= col)), g)


def _gla_scan_kernel(qf_ref, kf_ref, vf_ref, glf_ref, qb_ref, kb_ref, vb_ref, glb_ref, wgk_ref, bgk_ref,
                     of_ref, ob_ref, stf_ref, stb_ref, att_ref):
    @pl.when(pl.program_id(1) == 0)
    def _():
        stf_ref[...] = jnp.zeros_like(stf_ref)
        stb_ref[...] = jnp.zeros_like(stb_ref)

    ch = GLA_CHUNK
    rows, kd = qf_ref.shape
    nchunks = rows // ch
    g_f = _gla_gate_sums(glf_ref, wgk_ref, bgk_ref, slice(0, kd), reverse=False)
    g_b = _gla_gate_sums(glb_ref, wgk_ref, bgk_ref, slice(kd, 2 * kd), reverse=True)
    chunk_rows = [slice(i * ch, (i + 1) * ch) for i in range(nchunks)]
    for i in range(nchunks):
        _gla_scores(qf_ref, kf_ref, g_f[chunk_rows[i]], chunk_rows[i], att_ref, i, reverse=False)
        _gla_scores(qb_ref, kb_ref, g_b[chunk_rows[i]], chunk_rows[i], att_ref, nchunks + i, reverse=True)
    st_f = [stf_ref[h] for h in range(GLA_HEADS)]
    st_b = [stb_ref[h] for h in range(GLA_HEADS)]
    for i in range(nchunks):
        j = nchunks - 1 - i
        st_f = _gla_chunk(qf_ref, kf_ref, vf_ref, of_ref, g_f[chunk_rows[i]], st_f, chunk_rows[i], att_ref, i,
                          reverse=False)
        st_b = _gla_chunk(qb_ref, kb_ref, vb_ref, ob_ref, g_b[chunk_rows[j]], st_b, chunk_rows[j], att_ref,
                          nchunks + j, reverse=True)
    for h in range(GLA_HEADS):
        stf_ref[h] = st_f[h]
        stb_ref[h] = st_b[h]


def gla_scan(qkvg, gl, wgk, bgk, key_dim, value_dim):
    bsz, seq, _ = qkvg.shape
    ch = min(GLA_BLOCK, seq)
    nc = seq // ch
    fwd = lambda b, c: (b, c)
    bwd = lambda b, c: (b, nc - 1 - c)

    def specs(pos):
        return [pl.BlockSpec((None, ch, key_dim), lambda b, c: (*pos(b, c), 0)),
                pl.BlockSpec((None, ch, key_dim), lambda b, c: (*pos(b, c), 1)),
                pl.BlockSpec((None, ch, value_dim), lambda b, c: (*pos(b, c), (2 * key_dim) // value_dim)),
                pl.BlockSpec((None, ch, LANES), lambda b, c: (*pos(b, c), 0))]

    state = pltpu.VMEM((GLA_HEADS, key_dim // GLA_HEADS, value_dim // GLA_HEADS), F32)
    return pl.pallas_call(
        _gla_scan_kernel,
        grid=(bsz, nc),
        in_specs=specs(fwd) + specs(bwd) + [pl.BlockSpec((LANES, 2 * key_dim), lambda b, c: (0, 0)),
                                            pl.BlockSpec((1, 2 * key_dim), lambda b, c: (0, 0))],
        out_specs=[pl.BlockSpec((None, ch, value_dim), lambda b, c: (*fwd(b, c), 0)),
                   pl.BlockSpec((None, ch, value_dim), lambda b, c: (*bwd(b, c), 0))],
        out_shape=[jax.ShapeDtypeStruct((bsz, seq, value_dim), F32)] * 2,
        scratch_shapes=[state, state,
                        pltpu.VMEM((2 * (ch // GLA_CHUNK) * GLA_HEADS, GLA_CHUNK, GLA_CHUNK), MXU_DTYPE)],
        compiler_params=_cparams(("parallel", "arbitrary")),
        name="gla_scan",
    )(qkvg, qkvg, qkvg, gl, qkvg, qkvg, qkvg, gl, wgk, bgk)


def _gla_out_kernel(of_ref, ob_ref, hnw_ref, w_ref, res_ref, mnw_ref, wg_ref, o_ref):
    res = res_ref[...]
    g = _dot(_rms(res, mnw_ref[...]), wg_ref[...])
    o = of_ref[...] + ob_ref[...]
    dv = o.shape[1] // GLA_HEADS
    o = jnp.concatenate([_rms(o[:, h * dv:(h + 1) * dv], hnw_ref[...]) for h in range(GLA_HEADS)], axis=1)
    o_ref[...] = res + _dot(o * _silu(g), w_ref[...])


def gla_out(o_f, o_b, head_nw, out_w, res, mix_nw, w_g):
    t, vd = o_f.shape
    d = out_w.shape[1]
    tm = min(ROW_TILE, t)
    row = lambda i: (i, 0)
    const = lambda i: (0, 0)
    return pl.pallas_call(
        _gla_out_kernel,
        grid=(t // tm,),
        in_specs=[pl.BlockSpec((tm, vd), row), pl.BlockSpec((tm, vd), row),
                  pl.BlockSpec((1, vd // GLA_HEADS), const),
                  pl.BlockSpec((vd, d), const), pl.BlockSpec((tm, d), row),
                  pl.BlockSpec((1, d), const), pl.BlockSpec((d, vd), const)],
        out_specs=pl.BlockSpec((tm, d), row),
        out_shape=jax.ShapeDtypeStruct((t, d), F32),
        compiler_params=_cparams(("parallel",)),
        name="gla_out",
    )(o_f, o_b, head_nw, out_w, res, mix_nw, w_g)


def _hy_filter_kernel(w1_ref, b1_ref, w2_ref, b2_ref, w3_ref, freq_ref, delta_ref, o_ref, *, seq):
    tr = o_ref.shape[0]
    r = pl.program_id(1) * tr + lax.broadcasted_iota(jnp.int32, (tr, 1), 0)
    pos = jnp.where(r < seq, r, 2 * seq - r).astype(F32)
    t = pos / (seq - 1.0)
    w = (2.0 * math.pi) * pos / seq
    band = lax.broadcasted_iota(jnp.int32, (1, HY_BANDS), 1).astype(F32)
    bands = 1e-4 + band * ((HY_BANDS - 1 - 1e-4) / (HY_BANDS - 1))
    zw1 = (t * w1_ref[0:1, :] + _dot_exact(jnp.cos(bands * w), w1_ref[1:1 + HY_BANDS, :])
           + _dot_exact(-jnp.sin(bands * w), w1_ref[1 + HY_BANDS:1 + 2 * HY_BANDS, :]))
    freq = freq_ref[...]
    hid = jnp.sin(freq * (zw1 + b1_ref[...]))
    hid = jnp.sin(freq * (_dot_exact(hid, w2_ref[...]) + b2_ref[...]))
    filt = _dot_exact(hid, w3_ref[...]) * jnp.exp(-t * delta_ref[...])
    o_ref[...] = jnp.where(r == seq, 0.0, filt)


def hy_filter(seq, w1, b1, w2, b2, w3, freq, width, tr=512):
    n = 2 * seq
    tr = min(tr, seq)
    tc = width
    emb, hidden = w1.shape
    min_decay = math.log(HY_DECAY_TARGET) / HY_SLOW_DECAY_PCT
    max_decay = math.log(HY_DECAY_TARGET) / HY_FAST_DECAY_PCT
    deltas = jnp.abs(jnp.linspace(min_decay, max_decay, width, dtype=F32))[None, :]
    nct = width // tc
    half = seq // tr
    const = lambda j, i: (0, 0)
    return pl.pallas_call(
        functools.partial(_hy_filter_kernel, seq=seq),
        grid=(nct, n // tr),
        in_specs=[pl.BlockSpec((emb, hidden), const), pl.BlockSpec((1, hidden), const),
                  pl.BlockSpec((hidden, hidden), const), pl.BlockSpec((1, hidden), const),
                  pl.BlockSpec((hidden, tc), lambda j, i: (0, j + nct * (i // half))),
                  pl.BlockSpec((1, hidden), const),
                  pl.BlockSpec((1, tc), lambda j, i: (0, j))],
        out_specs=pl.BlockSpec((tr, tc), lambda j, i: (i, j)),
        out_shape=jax.ShapeDtypeStruct((n, width), F32),
        compiler_params=_cparams(("parallel", "parallel")),
        name="hy_filter",
    )(w1, b1, w2, b2, w3, freq, deltas)


def _stack_complex(m):
    return np.concatenate([np.concatenate([m.real, -m.imag], axis=-1),
                           np.concatenate([m.imag, m.real], axis=-1)], axis=-2)


@functools.lru_cache(maxsize=None)
def _dft_tables(n, n1):
    n2 = n // n1
    h1 = n1 // 2
    ang = -2.0 * np.pi / n
    k1 = np.arange(n1)[None, :, None]
    nn1 = np.arange(n1)[None, None, :]
    nn2 = np.arange(n2)[:, None, None]
    g1_full = np.exp(1j * ang * k1 * (n2 * nn1 + nn2))
    g1 = _stack_complex(g1_full[:, :, :h1])
    g1_real = np.concatenate([g1_full.real, g1_full.imag], axis=-2)
    kk2 = np.arange(n2)[:, None]
    f2 = np.exp(-2j * np.pi * kk2 * np.arange(n2)[None, :] / n2)
    f2s = _stack_complex(f2)
    f3s = _stack_complex(np.conj(f2))
    out1 = np.arange(h1)[None, :, None]
    g4 = np.exp(-1j * ang * (n2 * out1 + nn2) * np.arange(n1)[None, None, :]) / n
    g4s = _stack_complex(g4)
    cast = lambda a: jnp.asarray(a, dtype=F32).astype(MXU_DTYPE)
    return cast(g1), cast(g1_real), cast(f2s), cast(f3s), cast(g4s)


FFT_UNROLL = 8
FFT_SLABS = 4
FFT_PAD = 8


def _load_complex_strided(ref, start, rows, stride):
    return jnp.concatenate([ref[0, pl.ds(start, rows, stride=stride), :],
                            ref[1, pl.ds(start, rows, stride=stride), :]], axis=0)


def _store_complex_slab(ref, base, rows, out):
    ref[0, pl.ds(base, rows), :] = out[:rows]
    ref[1, pl.ds(base, rows), :] = out[rows:]


def _shared_matrix_stage(load, store, mat_ref, count):
    def body(jj, carry):
        j0 = jj * FFT_SLABS
        x = jnp.concatenate([load(j0 + u) for u in range(FFT_SLABS)], axis=1).astype(MXU_DTYPE)
        out = jnp.dot(mat_ref[...], x, preferred_element_type=F32)
        for u in range(FFT_SLABS):
            store(j0 + u, out[:, u * LANES:(u + 1) * LANES])
        return carry

    lax.fori_loop(0, count // FFT_SLABS, body, 0, unroll=2)


def _per_slab_matrix_stage(load, store, mats_ref, count):
    def body(j, carry):
        store(j, jnp.dot(mats_ref[j], load(j).astype(MXU_DTYPE), preferred_element_type=F32))
        return carry

    lax.fori_loop(0, count, body, 0, unroll=FFT_UNROLL)


def _slab_base(j, rows):
    return pl.multiple_of(j * (rows + FFT_PAD), 8)


def _scratch_rows(n, n1):
    n2 = n // n1
    return max(n2 * (n1 + FFT_PAD), n1 * (n2 + FFT_PAD))


def _hy_spectrum_kernel(filt_ref, g1r_ref, f2s_ref, hf_ref, a_ref, *, n1):
    n = filt_ref.shape[0]
    n2 = n // n1
    _per_slab_matrix_stage(
        lambda j: filt_ref[pl.ds(j, n1, stride=n2), :],
        lambda j, out: _store_complex_slab(a_ref, _slab_base(j, n1), n1, out),
        g1r_ref, n2)
    _shared_matrix_stage(
        lambda j: _load_complex_strided(a_ref, j, n2, n1 + FFT_PAD),
        lambda j, out: _store_complex_slab(hf_ref, pl.multiple_of(j * n2, n2), n2, out),
        f2s_ref, n1)


def hy_spectrum(filt):
    n, c = filt.shape
    n1 = FFT_N1
    _, g1r, f2s, _, _ = _dft_tables(n, n1)
    return pl.pallas_call(
        functools.partial(_hy_spectrum_kernel, n1=n1),
        grid=(c // LANES,),
        in_specs=[pl.BlockSpec((n, LANES), lambda j: (0, j)),
                  pl.BlockSpec(g1r.shape, lambda j: (0, 0, 0)),
                  pl.BlockSpec(f2s.shape, lambda j: (0, 0))],
        out_specs=pl.BlockSpec((2, n, LANES), lambda j: (0, 0, j)),
        out_shape=jax.ShapeDtypeStruct((2, n, c), F32),
        scratch_shapes=[pltpu.VMEM((2, _scratch_rows(n, n1), LANES), F32)],
        compiler_params=_cparams(("parallel",)),
        name="hy_spectrum",
    )(filt, g1r, f2s)


def _hy_fftconv_kernel(v_ref, hf_ref, g1_ref, f2s_ref, f3s_ref, g4s_ref, o_ref, a_ref, y_ref, *, n1):
    seq = v_ref.shape[1]
    n = 2 * seq
    n2 = n // n1
    h1 = n1 // 2

    def multiply_by_spectrum(j, out):
        base = pl.multiple_of(j * n2, n2)
        xr, xi = out[:n2], out[n2:]
        hr = hf_ref[0, pl.ds(base, n2), :]
        hi = hf_ref[1, pl.ds(base, n2), :]
        y_ref[0, pl.ds(base, n2), :] = xr * hr - xi * hi
        y_ref[1, pl.ds(base, n2), :] = xr * hi + xi * hr

    def stage_input(m, carry):
        for comp in range(2):
            y_ref[comp, pl.ds(_slab_base(m, n2), n2), :] = v_ref[comp, pl.ds(pl.multiple_of(m * n2, n2), n2), :]
        return carry

    lax.fori_loop(0, h1, stage_input, 0)

    def stage_output(j, out):
        _store_complex_slab(y_ref, _slab_base(j, h1), h1, out)

    def unstage_output(m, carry):
        for comp in range(2):
            o_ref[comp, pl.ds(pl.multiple_of(m * n2, n2), n2), :] = y_ref[comp, pl.ds(m, n2, stride=h1 + FFT_PAD), :]
        return carry

    _per_slab_matrix_stage(
        lambda j: _load_complex_strided(y_ref, j, h1, n2 + FFT_PAD),
        lambda j, out: _store_complex_slab(a_ref, _slab_base(j, n1), n1, out),
        g1_ref, n2)
    _shared_matrix_stage(lambda j: _load_complex_strided(a_ref, j, n2, n1 + FFT_PAD), multiply_by_spectrum,
                         f2s_ref, n1)
    _shared_matrix_stage(
        lambda j: jnp.concatenate([y_ref[0, pl.ds(pl.multiple_of(j * n2, n2), n2), :],
                                   y_ref[1, pl.ds(pl.multiple_of(j * n2, n2), n2), :]], axis=0),
        lambda j, out: _store_complex_slab(a_ref, _slab_base(j, n2), n2, out),
        f3s_ref, n1)
    _per_slab_matrix_stage(lambda j: _load_complex_strided(a_ref, j, n1, n2 + FFT_PAD), stage_output, g4s_ref, n2)
    lax.fori_loop(0, h1, unstage_output, 0)


def hy_fftconv(vx, hf):
    bsz, seq, c = vx.shape
    n = 2 * seq
    n1 = FFT_N1
    g1, _, f2s, f3s, g4s = _dft_tables(n, n1)
    once = pl.Buffered(1)
    tab3 = lambda a: pl.BlockSpec(a.shape, lambda j, p: (0, 0, 0), pipeline_mode=once)
    tab2 = lambda a: pl.BlockSpec(a.shape, lambda j, p: (0, 0), pipeline_mode=once)
    pair = pl.BlockSpec((None, 2, seq, LANES), lambda j, p: (p, 0, 0, j))
    out = pl.pallas_call(
        functools.partial(_hy_fftconv_kernel, n1=n1),
        grid=(c // LANES, bsz // 2),
        in_specs=[pair,
                  pl.BlockSpec((2, n, LANES), lambda j, p: (0, 0, j), pipeline_mode=once),
                  tab3(g1), tab2(f2s), tab2(f3s), tab3(g4s)],
        out_specs=pair,
        out_shape=jax.ShapeDtypeStruct((bsz // 2, 2, seq, c), F32),
        scratch_shapes=[pltpu.VMEM((2, _scratch_rows(n, n1), LANES), F32), pltpu.VMEM((2, n, LANES), F32)],
        compiler_params=_cparams(("parallel", "parallel")),
        name="hy_fftconv",
    )(vx.reshape(bsz // 2, 2, seq, c), hf, g1, f2s, f3s, g4s)
    return out.reshape(bsz, seq, c)


def _hy_out_kernel(conv_ref, vx_ref, x0_ref, skip_ref, w_ref, b_ref, res_ref, o_ref):
    y = (conv_ref[...] + vx_ref[...] * skip_ref[...]) * x0_ref[...]
    o_ref[...] = res_ref[...] + _dot(y, w_ref[...]) + b_ref[...]


def hy_out(conv, vx, x0c, skip, out_w, out_b, res):
    t, c = vx.shape
    d = out_w.shape[1]
    tm = min(ROW_TILE, t)
    row = lambda i: (i, 0)
    const = lambda i: (0, 0)
    return pl.pallas_call(
        _hy_out_kernel,
        grid=(t // tm,),
        in_specs=[pl.BlockSpec((tm, c), row), pl.BlockSpec((tm, c), row), pl.BlockSpec((tm, c), row),
                  pl.BlockSpec((1, c), const), pl.BlockSpec((c, d), const), pl.BlockSpec((1, d), const),
                  pl.BlockSpec((tm, d), row)],
        out_specs=pl.BlockSpec((tm, d), row),
        out_shape=jax.ShapeDtypeStruct((t, d), F32),
        compiler_params=_cparams(("parallel",)),
        name="hy_out",
    )(conv, vx, x0c, skip, out_w, out_b, res)


def _pad_cols(a, width):
    return jnp.pad(a, ((0, 0), (0, width - a.shape[1])))


def _col_tile(n):
    for tile in (1024, 768, 512, 256, 128):
        if n % tile == 0:
            return tile
    raise ValueError(f"unsupported projection width {n}")


def mamba2_mixer(h, bsz, norm_w, in_w, conv_w, conv_b, dt_bias, a_log, d_skip, gn_w, out_w):
    t, d = h.shape
    seq = t // bsz
    heads = a_log.shape[1]
    d_inner = heads * SSD_HEADDIM
    conv_dim = conv_w.shape[1]
    main = d_inner + conv_dim
    w_dt = _pad_cols(in_w[:, main:], LANES).astype(MXU_DTYPE)
    xbc, dt_raw = norm_linear_conv(h, seq, norm_w, in_w[:, d_inner:main].astype(MXU_DTYPE),
                                   jnp.zeros((1, conv_dim), F32), conv_w, conv_b[None, :], _col_tile(conv_dim),
                                   _silu_epilogue, (_col_tile(conv_dim),), w_side=w_dt)
    y_f, y_b = ssd_scan(xbc.reshape(bsz, seq, conv_dim), dt_raw.reshape(bsz, seq, LANES),
                        _pad_cols(dt_bias.reshape(1, 2 * heads), LANES),
                        _pad_cols(a_log.reshape(1, 2 * heads), LANES))
    return ssd_out(y_f.reshape(t, d_inner), y_b.reshape(t, d_inner), xbc, jnp.repeat(d_skip, SSD_HEADDIM)[None, :],
                   gn_w[None, :], out_w.astype(MXU_DTYPE), h, norm_w, in_w[:, :d_inner].astype(MXU_DTYPE))


def hyena_mixer(h, bsz, norm_w, in_w, in_b, conv_w, conv_b, f_w1, f_b1, f_w2, f_b2, f_w3, sin_freq, skip,
                out_w, out_b):
    t, d = h.shape
    seq = t // bsz
    width = skip.shape[0]
    sub = HY_PART_TILE
    by_tile = lambda a: a.reshape(a.shape[0], 3, width // sub, sub).transpose(0, 2, 1, 3).reshape(a.shape[0], 3 * width)
    x0c, vx = norm_linear_conv(h, seq, norm_w, by_tile(in_w).astype(MXU_DTYPE), by_tile(in_b[None, :]),
                               by_tile(conv_w), by_tile(conv_b[None, :]), 3 * sub, _hyena_epilogue, (sub, sub))
    filt = hy_filter(seq, f_w1, f_b1[None, :], f_w2, f_b2[None, :], f_w3, sin_freq[None, :], width)
    conv = hy_fftconv(vx.reshape(bsz, seq, width), hy_spectrum(filt))
    return hy_out(conv.reshape(t, width), vx, x0c, skip[None, :], out_w.astype(MXU_DTYPE), out_b[None, :], h)


def gla_mixer(h, bsz, norm_w, in_w, gk_w, gk_b, hn_w, out_w):
    t, d = h.shape
    seq = t // bsz
    rank, key_dim = gk_w.shape[1], gk_w.shape[2]
    value_dim = out_w.shape[0]
    qkv_dim = 2 * key_dim + value_dim
    main = qkv_dim + value_dim
    w_gl = _pad_cols(in_w[:, main:], LANES).astype(MXU_DTYPE)
    qkv, gl = norm_linear(h, norm_w, in_w[:, :qkv_dim].astype(MXU_DTYPE), w_gl, _col_tile(qkv_dim))
    wgk = jnp.zeros((LANES, 2 * key_dim), F32)
    wgk = wgk.at[:rank, :key_dim].set(gk_w[0]).at[rank:2 * rank, key_dim:].set(gk_w[1]).astype(MXU_DTYPE)
    o_f, o_b = gla_scan(qkv.reshape(bsz, seq, qkv_dim), gl.reshape(bsz, seq, LANES), wgk,
                        gk_b.reshape(1, 2 * key_dim), key_dim, value_dim)
    return gla_out(o_f.reshape(t, value_dim), o_b.reshape(t, value_dim), hn_w[None, :], out_w.astype(MXU_DTYPE), h,
                   norm_w, in_w[:, qkv_dim:main].astype(MXU_DTYPE))


def kernel(x, p, norm_mix, norm_ffn, norm_ple, ple_gate, ple_proj, ffn_w1, ffn_w3, ffn_w2, final_norm,
           ssd_in_w, ssd_conv_w, ssd_conv_b, ssd_dt_bias, ssd_a_log, ssd_d, ssd_norm, ssd_out_w,
           hy_in_w, hy_in_b, hy_conv_w, hy_conv_b, hy_f_w1, hy_f_b1, hy_f_w2, hy_f_b2, hy_f_w3,
           hy_sin_freq, hy_skip, hy_out_w, hy_out_b,
           gla_in_w, gla_gk_w, gla_gk_b, gla_norm, gla_out_w):
    bsz, seq, d = x.shape
    depth = p.shape[0]
    t = bsz * seq
    h = x.reshape(t, d)
    for i in range(depth):
        kind, j = i % N_MIXERS, i // N_MIXERS
        nw = norm_mix[i][None, :]
        if kind == 0:
            h = mamba2_mixer(h, bsz, nw, ssd_in_w[j], ssd_conv_w[j], ssd_conv_b[j], ssd_dt_bias[j], ssd_a_log[j],
                             ssd_d[j], ssd_norm[j], ssd_out_w[j])
        elif kind == 1:
            h = hyena_mixer(h, bsz, nw, hy_in_w[j], hy_in_b[j], hy_conv_w[j], hy_conv_b[j], hy_f_w1[j], hy_f_b1[j],
                            hy_f_w2[j], hy_f_b2[j], hy_f_w3[j], hy_sin_freq[j], hy_skip[j], hy_out_w[j], hy_out_b[j])
        else:
            h = gla_mixer(h, bsz, nw, gla_in_w[j], gla_gk_w[j], gla_gk_b[j], gla_norm[j], gla_out_w[j])
        h = ffn_ple(h, norm_ffn[i][None, :], ffn_w1[i].astype(MXU_DTYPE), ffn_w3[i].astype(MXU_DTYPE),
                    ffn_w2[i].astype(MXU_DTYPE), p[i].reshape(t, -1), norm_ple[i][None, :],
                    ple_gate[i].astype(MXU_DTYPE), ple_proj[i].astype(MXU_DTYPE), final_norm[None, :],
                    final=(i == depth - 1))
    return h.reshape(bsz, seq, d)
```

```python
import functools
import math

import numpy as np
import jax
import jax.numpy as jnp
from jax import lax
from jax.experimental import pallas as pl
from jax.experimental.pallas import tpu as pltpu

F32 = jnp.float32
MXU_DTYPE = jnp.bfloat16
EXACT = lax.Precision.HIGHEST

NORM_EPS = 1e-6
N_MIXERS = 3

VMEM_LIMIT_BYTES = 56 * 1024 * 1024
ROW_TILE = 512
PROJ_ROW_TILE = 1024
LANES = 128
FFN_SLICE = 512

SSD_HEADDIM = 64
SSD_GROUPS = 8
SSD_STATE = 128
SSD_CHUNK = 128
SSD_PAIR = 2 * SSD_HEADDIM
assert SSD_PAIR == LANES == SSD_CHUNK
GLA_HEADS = 4
GLA_CHUNK = 64
GLA_BLOCK = 256
GLA_SUB = 16
GLA_GATE_RANK = 16
GLA_GATE_NORMALIZER = 16.0
HY_PART_TILE = 256
HY_BANDS = 16
HY_DECAY_TARGET = 1e-2
HY_FAST_DECAY_PCT = 0.3
HY_SLOW_DECAY_PCT = 1.5
FFT_N1 = 64


def _cparams(sem):
    return pltpu.CompilerParams(dimension_semantics=sem, vmem_limit_bytes=VMEM_LIMIT_BYTES)


def _dot(a, b):
    return jnp.dot(a.astype(MXU_DTYPE), b.astype(MXU_DTYPE), preferred_element_type=F32)


def _dot_nt(a, b):
    return lax.dot_general(a.astype(MXU_DTYPE), b.astype(MXU_DTYPE), (((1,), (1,)), ((), ())),
                           preferred_element_type=F32)


def _dot_tn(a, b):
    return lax.dot_general(a.astype(MXU_DTYPE), b.astype(MXU_DTYPE), (((0,), (0,)), ((), ())),
                           preferred_element_type=F32)


def _dot_exact(a, b):
    return jnp.dot(a, b, preferred_element_type=F32, precision=EXACT)


def _split3(x):
    hi = x.astype(MXU_DTYPE)
    rest = x - hi.astype(F32)
    mid = rest.astype(MXU_DTYPE)
    lo = (rest - mid.astype(F32)).astype(MXU_DTYPE)
    return hi, mid, lo


def _mask_times(mask, x):
    m = jnp.where(mask, 1.0, 0.0).astype(MXU_DTYPE)
    return jnp.dot(jnp.concatenate([m, m, m], axis=1), jnp.concatenate(_split3(x), axis=0),
                   preferred_element_type=F32)


def _times_mask(x, mask):
    m = jnp.where(mask, 1.0, 0.0).astype(MXU_DTYPE)
    return jnp.dot(jnp.concatenate(_split3(x), axis=1), jnp.concatenate([m, m, m], axis=0),
                   preferred_element_type=F32)


def _rms(x, w):
    return x * lax.rsqrt(jnp.mean(x * x, axis=-1, keepdims=True) + NORM_EPS) * w


def _sigmoid(x):
    return 1.0 / (1.0 + jnp.exp(-x))


def _silu(x):
    return x * _sigmoid(x)


def _softplus(x):
    return jnp.maximum(x, 0.0) + jnp.log(1.0 + jnp.exp(-jnp.abs(x)))


def _norm_linear_kernel(x_ref, nw_ref, w_ref, ws_ref, o_ref, os_ref, xn_ref):
    @pl.when(pl.program_id(1) == 0)
    def _():
        xn_ref[...] = _rms(x_ref[...], nw_ref[...]).astype(xn_ref.dtype)
        os_ref[...] = jnp.dot(xn_ref[...], ws_ref[...], preferred_element_type=F32)

    o_ref[...] = jnp.dot(xn_ref[...], w_ref[...], preferred_element_type=F32)


def norm_linear(x, nw, w, w_side, col_tile):
    t, d = x.shape
    n = w.shape[1]
    tm = min(PROJ_ROW_TILE, t)
    return pl.pallas_call(
        _norm_linear_kernel,
        grid=(t // tm, n // col_tile),
        in_specs=[pl.BlockSpec((tm, d), lambda i, j: (i, 0)),
                  pl.BlockSpec((1, d), lambda i, j: (0, 0)),
                  pl.BlockSpec((d, col_tile), lambda i, j: (0, j)),
                  pl.BlockSpec((d, LANES), lambda i, j: (0, 0))],
        out_specs=[pl.BlockSpec((tm, col_tile), lambda i, j: (i, j)),
                   pl.BlockSpec((tm, LANES), lambda i, j: (i, 0))],
        out_shape=[jax.ShapeDtypeStruct((t, n), F32), jax.ShapeDtypeStruct((t, LANES), F32)],
        scratch_shapes=[pltpu.VMEM((tm, d), MXU_DTYPE)],
        compiler_params=_cparams(("parallel", "arbitrary")),
        name="norm_linear",
    )(x, nw, w, w_side)


def _ffn_ple_kernel(h_ref, nf_ref, w1_ref, w3_ref, w2_ref, p_ref, npl_ref, g_ref, pp_ref, fn_ref,
                    o_ref, xn_ref, acc_ref, *, final):
    f = pl.program_id(1)

    @pl.when(f == 0)
    def _():
        xn_ref[...] = _rms(h_ref[...], nf_ref[...]).astype(xn_ref.dtype)
        acc_ref[...] = jnp.zeros_like(acc_ref)

    xn = xn_ref[...]
    tf = w1_ref.shape[1]
    total = None
    for c0 in range(0, tf, FFN_SLICE):
        cs = slice(c0, min(c0 + FFN_SLICE, tf))
        a = jnp.dot(xn, w1_ref[:, cs], preferred_element_type=F32)
        b = jnp.dot(xn, w3_ref[:, cs], preferred_element_type=F32)
        part = _dot(_silu(a) * b, w2_ref[cs, :])
        total = part if total is None else total + part
    acc_ref[...] += total

    @pl.when(f == pl.num_programs(1) - 1)
    def _():
        h2 = h_ref[...] + acc_ref[...]
        gate = _sigmoid(_dot(_rms(h2, npl_ref[...]), g_ref[...]))
        h3 = h2 + gate * _dot(p_ref[...], pp_ref[...])
        if final:
            h3 = _rms(h3, fn_ref[...])
        o_ref[...] = h3


def ffn_ple(h, nf, w1, w3, w2, p, npl, g, pp, fn, final):
    t, d = h.shape
    dff = w1.shape[1]
    tf = dff // 2
    pdim = p.shape[1]
    tm = min(ROW_TILE, t)
    row = lambda i, f: (i, 0)
    const = lambda i, f: (0, 0)
    return pl.pallas_call(
        functools.partial(_ffn_ple_kernel, final=final),
        grid=(t // tm, dff // tf),
        in_specs=[pl.BlockSpec((tm, d), row),
                  pl.BlockSpec((1, d), const),
                  pl.BlockSpec((d, tf), lambda i, f: (0, f)),
                  pl.BlockSpec((d, tf), lambda i, f: (0, f)),
                  pl.BlockSpec((tf, d), lambda i, f: (f, 0)),
                  pl.BlockSpec((tm, pdim), row),
                  pl.BlockSpec((1, d), const),
                  pl.BlockSpec((d, d), const),
                  pl.BlockSpec((pdim, d), const),
                  pl.BlockSpec((1, d), const)],
        out_specs=pl.BlockSpec((tm, d), row),
        out_shape=jax.ShapeDtypeStruct((t, d), F32),
        scratch_shapes=[pltpu.VMEM((tm, d), MXU_DTYPE), pltpu.VMEM((tm, d), F32)],
        compiler_params=_cparams(("parallel", "arbitrary")),
        name="ffn_ple",
    )(h, nf, w1, w3, w2, p, npl, g, pp, fn)


CONV_HALO = 16
CONV_ROWS = 256


def _norm_linear_conv_kernel(xp_ref, x_ref, xn_ref, nw_ref, w_ref, cw_ref, cb_ref, *rest,
                             taps, tiles_per_seq, epilogue, has_bias, side):
    b_ref, rest = (rest[0], rest[1:]) if has_bias else (None, rest)
    ws_ref, rest = (rest[0], rest[1:]) if side else (None, rest)
    *out_refs, xs_ref = rest
    os_ref = out_refs.pop() if side else None
    tm = x_ref.shape[0]
    i = pl.program_id(0)

    @pl.when(pl.program_id(1) == 0)
    def _():
        nw = nw_ref[...]
        xs_ref[0:CONV_HALO, :] = _rms(xp_ref[...], nw).astype(xs_ref.dtype)
        xs_ref[CONV_HALO:CONV_HALO + tm, :] = _rms(x_ref[...], nw).astype(xs_ref.dtype)
        xs_ref[CONV_HALO + tm:, :] = _rms(xn_ref[...], nw).astype(xs_ref.dtype)
        if side:
            os_ref[...] = jnp.dot(xs_ref[CONV_HALO:CONV_HALO + tm, :], ws_ref[...], preferred_element_type=F32)

    first = (i % tiles_per_seq) == 0
    last = (i % tiles_per_seq) == tiles_per_seq - 1
    rows = min(CONV_ROWS, tm)
    nchunks = tm // rows
    for r in range(nchunks):
        base = r * rows
        win = jnp.dot(xs_ref[base:base + rows + 2 * CONV_HALO, :], w_ref[...],
                      preferred_element_type=F32)
        if has_bias:
            win = win + b_ref[...]
        top, bottom = win[:CONV_HALO], win[CONV_HALO + rows:]
        if r == 0:
            top = jnp.where(first, 0.0, top)
        if r == nchunks - 1:
            bottom = jnp.where(last, 0.0, bottom)
        win = jnp.concatenate([top, win[CONV_HALO:CONV_HALO + rows], bottom], axis=0)
        acc = cb_ref[...]
        for k in range(taps):
            shift = (taps // 2 - k) % win.shape[0]
            moved = pltpu.roll(win, shift, 0) if shift else win
            acc = acc + cw_ref[k:k + 1, :] * moved[CONV_HALO:CONV_HALO + rows, :]
        epilogue(out_refs, slice(base, base + rows), acc)


def _silu_epilogue(out_refs, rows, acc):
    out_refs[0][rows, :] = _silu(acc)


def _hyena_epilogue(out_refs, rows, acc):
    c = acc.shape[1] // 3
    out_refs[0][rows, :] = acc[:, :c]
    out_refs[1][rows, :] = acc[:, 2 * c:] * acc[:, c:2 * c]


def norm_linear_conv(x, seq, nw, w, b, conv_w, conv_b, col_tile, epilogue, out_widths, w_side=None):
    t, d = x.shape
    n = w.shape[1]
    taps = conv_w.shape[0]
    tm = min(PROJ_ROW_TILE, seq)
    per_tile = tm // CONV_HALO
    last_halo = t // CONV_HALO - 1
    side = w_side is not None
    has_bias = b is not None
    in_specs = [pl.BlockSpec((CONV_HALO, d), lambda i, j: (jnp.maximum(i * per_tile - 1, 0), 0)),
                pl.BlockSpec((tm, d), lambda i, j: (i, 0)),
                pl.BlockSpec((CONV_HALO, d), lambda i, j: (jnp.minimum((i + 1) * per_tile, last_halo), 0)),
                pl.BlockSpec((1, d), lambda i, j: (0, 0)),
                pl.BlockSpec((d, col_tile), lambda i, j: (0, j)),
                pl.BlockSpec((taps, col_tile), lambda i, j: (0, j)),
                pl.BlockSpec((1, col_tile), lambda i, j: (0, j))]
    out_specs = [pl.BlockSpec((tm, ow), lambda i, j: (i, j)) for ow in out_widths]
    out_shape = [jax.ShapeDtypeStruct((t, n // col_tile * ow), F32) for ow in out_widths]
    args = [x, x, x, nw, w, conv_w, conv_b]
    if has_bias:
        in_specs.append(pl.BlockSpec((1, col_tile), lambda i, j: (0, j)))
        args.append(b)
    if side:
        in_specs.append(pl.BlockSpec((d, LANES), lambda i, j: (0, 0)))
        out_specs.append(pl.BlockSpec((tm, LANES), lambda i, j: (i, 0)))
        out_shape.append(jax.ShapeDtypeStruct((t, LANES), F32))
        args.append(w_side)
    return pl.pallas_call(
        functools.partial(_norm_linear_conv_kernel, taps=taps, tiles_per_seq=seq // tm, epilogue=epilogue,
                          has_bias=has_bias, side=side),
        grid=(t // tm, n // col_tile),
        in_specs=in_specs,
        out_specs=out_specs,
        out_shape=out_shape,
        scratch_shapes=[pltpu.VMEM((tm + 2 * CONV_HALO, d), MXU_DTYPE)],
        compiler_params=_cparams(("parallel", "arbitrary")),
        name="norm_linear_conv",
    )(*args)


def _tri(n, reverse):
    row = lax.broadcasted_iota(jnp.int32, (n, n), 0)
    col = lax.broadcasted_iota(jnp.int32, (n, n), 1)
    return (row <= col) if reverse else (row >= col)


def _ssd_decays(dt_ref, dtb_ref, alog_ref, *, heads, lane0, reverse):
    t = SSD_CHUNK
    dt_all = _softplus(dt_ref[...] + dtb_ref[...])
    da_all = dt_all * (-jnp.exp(alog_ref[...]))
    da = da_all[:, lane0:lane0 + heads]
    dt_t = dt_all.T[lane0:lane0 + heads, :]
    da_t = da_all.T[lane0:lane0 + heads, :]
    acum = _mask_times(_tri(t, reverse), da)
    acum_t = _times_mask(da_t, _tri(t, not reverse))
    tot_c = acum_t[:, 0:1] if reverse else acum_t[:, t - 1:t]
    w_t = jnp.exp(tot_c - acum_t) * dt_t
    src_t = acum_t - jnp.log(dt_t)
    etot = jnp.exp(jnp.sum(da, axis=0, keepdims=True))
    return acum, src_t, w_t, etot


def _ssd_group(g, xs_ref, b_ref, c_ref, st_ref, y_ref, decays, *, reverse):
    t = SSD_CHUNK
    acum, src_t, w_t, etot = decays
    per_group = xs_ref.shape[1] // SSD_HEADDIM // SSD_GROUPS
    gw = per_group * SSD_HEADDIM
    tri = _tri(t, reverse)
    first = lax.broadcasted_iota(jnp.int32, (t, SSD_PAIR), 1) < SSD_HEADDIM
    bg_t = b_ref[:, g * SSD_STATE:(g + 1) * SSD_STATE].T
    cg = c_ref[:, g * SSD_STATE:(g + 1) * SSD_STATE].astype(MXU_DTYPE)
    cb = _dot(cg, bg_t)
    st = st_ref[g]
    y_intra, e_acum, st_new = [], [], []
    for pr in range(per_group // 2):
        h0 = g * per_group + 2 * pr
        x_pair = xs_ref[:, h0 * SSD_HEADDIM:(h0 + 2) * SSD_HEADDIM]
        rhs = jnp.concatenate([jnp.where(first, x_pair, 0.0),
                               jnp.where(first, 0.0, x_pair)], axis=0).astype(MXU_DTYPE)
        cols = [jnp.broadcast_to(acum[:, h:h + 1], (t, SSD_PAIR)) for h in (h0, h0 + 1)]
        scores = [cb * jnp.exp(jnp.where(tri, cols[k] - src_t[h0 + k:h0 + k + 1, :], -jnp.inf)) for k in range(2)]
        y_intra.append(_dot(jnp.concatenate(scores, axis=1), rhs))
        e_acum.append(jnp.where(first, jnp.exp(cols[0]), jnp.exp(cols[1])))
        e_tot = jnp.where(first[0:1], jnp.broadcast_to(etot[:, h0:h0 + 1], (1, SSD_PAIR)),
                          jnp.broadcast_to(etot[:, h0 + 1:h0 + 2], (1, SSD_PAIR)))
        b_weighted = jnp.concatenate([bg_t * w_t[h0:h0 + 1, :], bg_t * w_t[h0 + 1:h0 + 2, :]], axis=1)
        st_new.append(st[:, pr * SSD_PAIR:(pr + 1) * SSD_PAIR] * e_tot + _dot(b_weighted, rhs))
    y_ref[:, g * gw:(g + 1) * gw] = (_dot(cg, st) * jnp.concatenate(e_acum, axis=1)
                                     + jnp.concatenate(y_intra, axis=1))
    st_ref[g] = jnp.concatenate(st_new, axis=1)


def _ssd_scan_kernel(xf_ref, bf_ref, cf_ref, dtf_ref, xb_ref, bb_ref, cb_ref, dtb_ref,
                     bias_ref, alog_ref, yf_ref, yb_ref, stf_ref, stb_ref):
    @pl.when(pl.program_id(1) == 0)
    def _():
        stf_ref[...] = jnp.zeros_like(stf_ref)
        stb_ref[...] = jnp.zeros_like(stb_ref)

    heads = xf_ref.shape[1] // SSD_HEADDIM
    dec_f = _ssd_decays(dtf_ref, bias_ref, alog_ref, heads=heads, lane0=0, reverse=False)
    dec_b = _ssd_decays(dtb_ref, bias_ref, alog_ref, heads=heads, lane0=heads, reverse=True)
    for g in range(SSD_GROUPS):
        _ssd_group(g, xf_ref, bf_ref, cf_ref, stf_ref, yf_ref, dec_f, reverse=False)
        _ssd_group(g, xb_ref, bb_ref, cb_ref, stb_ref, yb_ref, dec_b, reverse=True)


def ssd_scan(xbc, dt_raw, dt_bias, a_log):
    bsz, seq, width = xbc.shape
    gn = SSD_GROUPS * SSD_STATE
    d_inner = width - 2 * gn
    t = SSD_CHUNK
    nc = seq // t
    xblk = d_inner // gn
    fwd = lambda b, c: (b, c)
    bwd = lambda b, c: (b, nc - 1 - c)

    def specs(pos):
        return [pl.BlockSpec((None, t, d_inner), lambda b, c: (*pos(b, c), 0)),
                pl.BlockSpec((None, t, gn), lambda b, c: (*pos(b, c), xblk)),
                pl.BlockSpec((None, t, gn), lambda b, c: (*pos(b, c), xblk + 1)),
                pl.BlockSpec((None, t, LANES), lambda b, c: (*pos(b, c), 0))]

    const = pl.BlockSpec((1, LANES), lambda b, c: (0, 0))
    state = pltpu.VMEM((SSD_GROUPS, SSD_STATE, d_inner // SSD_GROUPS), F32)
    return pl.pallas_call(
        _ssd_scan_kernel,
        grid=(bsz, nc),
        in_specs=specs(fwd) + specs(bwd) + [const, const],
        out_specs=[pl.BlockSpec((None, t, d_inner), lambda b, c: (*fwd(b, c), 0)),
                   pl.BlockSpec((None, t, d_inner), lambda b, c: (*bwd(b, c), 0))],
        out_shape=[jax.ShapeDtypeStruct((bsz, seq, d_inner), F32)] * 2,
        scratch_shapes=[state, state],
        compiler_params=_cparams(("parallel", "arbitrary")),
        name="ssd_scan",
    )(xbc, xbc, xbc, dt_raw, xbc, xbc, xbc, dt_raw, dt_bias, a_log)


def _ssd_out_kernel(yf_ref, yb_ref, xs_ref, dsk_ref, gnw_ref, w_ref, res_ref, mnw_ref, wz_ref, o_ref):
    res = res_ref[...]
    z = _dot(_rms(res, mnw_ref[...]), wz_ref[...])
    y = (yf_ref[...] + yb_ref[...] + xs_ref[...] * dsk_ref[...]) * _silu(z)
    gw = y.shape[1] // SSD_GROUPS
    y = jnp.concatenate([_rms(y[:, g * gw:(g + 1) * gw], gnw_ref[:, g * gw:(g + 1) * gw])
                         for g in range(SSD_GROUPS)], axis=1)
    o_ref[...] = res + _dot(y, w_ref[...])


def ssd_out(y_f, y_b, xbc, d_skip, gn_w, out_w, res, mix_nw, w_z):
    t, d_inner = y_f.shape
    d = out_w.shape[1]
    tm = min(ROW_TILE, t)
    row = lambda i: (i, 0)
    const = lambda i: (0, 0)
    once = pl.Buffered(1)
    return pl.pallas_call(
        _ssd_out_kernel,
        grid=(t // tm,),
        in_specs=[pl.BlockSpec((tm, d_inner), row), pl.BlockSpec((tm, d_inner), row),
                  pl.BlockSpec((tm, d_inner), row),
                  pl.BlockSpec((1, d_inner), const), pl.BlockSpec((1, d_inner), const),
                  pl.BlockSpec((d_inner, d), const, pipeline_mode=once), pl.BlockSpec((tm, d), row),
                  pl.BlockSpec((1, d), const), pl.BlockSpec((d, d_inner), const, pipeline_mode=once)],
        out_specs=pl.BlockSpec((tm, d), row),
        out_shape=jax.ShapeDtypeStruct((t, d), F32),
        compiler_params=_cparams(("parallel",)),
        name="ssd_out",
    )(y_f, y_b, xbc, d_skip, gn_w, out_w, res, mix_nw, w_z)


def _pad_rows(x, r0, total):
    parts = []
    if r0 > 0:
        parts.append(jnp.zeros((r0, x.shape[1]), x.dtype))
    parts.append(x)
    if r0 + x.shape[0] < total:
        parts.append(jnp.zeros((total - r0 - x.shape[0], x.shape[1]), x.dtype))
    return jnp.concatenate(parts, axis=0) if len(parts) > 1 else x


def _gla_chunk(q_ref, k_ref, v_ref, o_ref, bc, states, rs, att_ref, slot, *, reverse):
    ch = GLA_CHUNK
    kd = q_ref.shape[1]
    dk = kd // GLA_HEADS
    dv = v_ref.shape[1] // GLA_HEADS
    eye = lax.broadcasted_iota(jnp.int32, (dk, dk), 0) == lax.broadcasted_iota(jnp.int32, (dk, dk), 1)
    tot = bc[0:1] if reverse else bc[ch - 1:ch]
    v = v_ref[rs, :].astype(MXU_DTYPE)
    q_in = q_ref[rs, :] * (dk ** -0.5) * jnp.exp(bc)
    k_st = k_ref[rs, :] * jnp.exp(tot - bc)
    e_tot = jnp.exp(tot)
    new_states = []
    for h in range(GLA_HEADS):
        ks = slice(h * dk, (h + 1) * dk)
        vs = slice(h * dv, (h + 1) * dv)
        st = states[h]
        o_ref[rs, vs] = _dot(jnp.concatenate([q_in[:, ks], att_ref[slot * GLA_HEADS + h]], axis=1),
                             jnp.concatenate([st.astype(MXU_DTYPE), v[:, vs]], axis=0))
        e_col = jnp.sum(jnp.where(eye, jnp.broadcast_to(e_tot[:, ks], (dk, dk)), 0.0), axis=1, keepdims=True)
        new_states.append(st * e_col + _dot_tn(k_st[:, ks], v[:, vs]))
    return new_states


def _gla_scores(q_ref, k_ref, bc, rs, att_ref, slot, *, reverse):
    ch = GLA_CHUNK
    dk = q_ref.shape[1] // GLA_HEADS
    nsub = ch // GLA_SUB
    tri = _tri(ch, reverse)
    q = q_ref[rs, :] * (dk ** -0.5)
    k = k_ref[rs, :]
    for h in range(GLA_HEADS):
        ks = slice(h * dk, (h + 1) * dk)
        bch, qh, kh = bc[:, ks], q[:, ks], k[:, ks]
        q_segs, k_segs = [], []
        for i in range(nsub):
            l0, l1 = i * GLA_SUB, (i + 1) * GLA_SUB
            if reverse:
                s0, s1 = l0, ch
                ref = bch[l1:l1 + 1] if l1 < ch else 0.0
            else:
                s0, s1 = 0, l1
                ref = bch[l0 - 1:l0] if l0 > 0 else 0.0
            q_segs.append(_pad_rows(qh[l0:l1] * jnp.exp(bch[l0:l1] - ref), l0, ch))
            k_segs.append(_pad_rows(kh[s0:s1] * jnp.exp(ref - bch[s0:s1]), s0, ch))
        att = _dot_nt(jnp.concatenate(q_segs, axis=1), jnp.concatenate(k_segs, axis=1))
        att_ref[slot * GLA_HEADS + h] = jnp.where(tri, att, 0.0).astype(att_ref.dtype)


def _gla_gate_sums(gl_ref, wgk_ref, bgk_ref, cols, reverse):
    x = _dot(gl_ref[...], wgk_ref[:, cols]) + bgk_ref[:, cols]
    g = -_softplus(-x) * (1.0 / GLA_GATE_NORMALIZER)
    rows = g.shape[0]
    row = lax.broadcasted_iota(jnp.int32, (rows, rows), 0)
    col = lax.broadcasted_iota(jnp.int32, (rows, rows), 1)
    same_chunk = (row // GLA_CHUNK) == (col // GLA_CHUNK)
    return _mask_times(same_chunk & ((row <= col) if reverse else (row >= col)), g)


def _gla_scan_kernel(qf_ref, kf_ref, vf_ref, glf_ref, qb_ref, kb_ref, vb_ref, glb_ref, wgk_ref, bgk_ref,
                     of_ref, ob_ref, stf_ref, stb_ref, att_ref):
    @pl.when(pl.program_id(1) == 0)
    def _():
        stf_ref[...] = jnp.zeros_like(stf_ref)
        stb_ref[...] = jnp.zeros_like(stb_ref)

    ch = GLA_CHUNK
    rows, kd = qf_ref.shape
    nchunks = rows // ch
    g_f = _gla_gate_sums(glf_ref, wgk_ref, bgk_ref, slice(0, kd), reverse=False)
    g_b = _gla_gate_sums(glb_ref, wgk_ref, bgk_ref, slice(kd, 2 * kd), reverse=True)
    chunk_rows = [slice(i * ch, (i + 1) * ch) for i in range(nchunks)]
    for i in range(nchunks):
        _gla_scores(qf_ref, kf_ref, g_f[chunk_rows[i]], chunk_rows[i], att_ref, i, reverse=False)
        _gla_scores(qb_ref, kb_ref, g_b[chunk_rows[i]], chunk_rows[i], att_ref, nchunks + i, reverse=True)
    st_f = [stf_ref[h] for h in range(GLA_HEADS)]
    st_b = [stb_ref[h] for h in range(GLA_HEADS)]
    for i in range(nchunks):
        j = nchunks - 1 - i
        st_f = _gla_chunk(qf_ref, kf_ref, vf_ref, of_ref, g_f[chunk_rows[i]], st_f, chunk_rows[i], att_ref, i,
                          reverse=False)
        st_b = _gla_chunk(qb_ref, kb_ref, vb_ref, ob_ref, g_b[chunk_rows[j]], st_b, chunk_rows[j], att_ref,
                          nchunks + j, reverse=True)
    for h in range(GLA_HEADS):
        stf_ref[h] = st_f[h]
        stb_ref[h] = st_b[h]


def gla_scan(qkvg, gl, wgk, bgk, key_dim, value_dim):
    bsz, seq, _ = qkvg.shape
    ch = min(GLA_BLOCK, seq)
    nc = seq // ch
    fwd = lambda b, c: (b, c)
    bwd = lambda b, c: (b, nc - 1 - c)

    def specs(pos):
        return [pl.BlockSpec((None, ch, key_dim), lambda b, c: (*pos(b, c), 0)),
                pl.BlockSpec((None, ch, key_dim), lambda b, c: (*pos(b, c), 1)),
                pl.BlockSpec((None, ch, value_dim), lambda b, c: (*pos(b, c), (2 * key_dim) // value_dim)),
                pl.BlockSpec((None, ch, LANES), lambda b, c: (*pos(b, c), 0))]

    state = pltpu.VMEM((GLA_HEADS, key_dim // GLA_HEADS, value_dim // GLA_HEADS), F32)
    return pl.pallas_call(
        _gla_scan_kernel,
        grid=(bsz, nc),
        in_specs=specs(fwd) + specs(bwd) + [pl.BlockSpec((LANES, 2 * key_dim), lambda b, c: (0, 0)),
                                            pl.BlockSpec((1, 2 * key_dim), lambda b, c: (0, 0))],
        out_specs=[pl.BlockSpec((None, ch, value_dim), lambda b, c: (*fwd(b, c), 0)),
                   pl.BlockSpec((None, ch, value_dim), lambda b, c: (*bwd(b, c), 0))],
        out_shape=[jax.ShapeDtypeStruct((bsz, seq, value_dim), F32)] * 2,
        scratch_shapes=[state, state,
                        pltpu.VMEM((2 * (ch // GLA_CHUNK) * GLA_HEADS, GLA_CHUNK, GLA_CHUNK), MXU_DTYPE)],
        compiler_params=_cparams(("parallel", "arbitrary")),
        name="gla_scan",
    )(qkvg, qkvg, qkvg, gl, qkvg, qkvg, qkvg, gl, wgk, bgk)


def _gla_out_kernel(of_ref, ob_ref, hnw_ref, w_ref, res_ref, mnw_ref, wg_ref, o_ref):
    res = res_ref[...]
    g = _dot(_rms(res, mnw_ref[...]), wg_ref[...])
    o = of_ref[...] + ob_ref[...]
    dv = o.shape[1] // GLA_HEADS
    o = jnp.concatenate([_rms(o[:, h * dv:(h + 1) * dv], hnw_ref[...]) for h in range(GLA_HEADS)], axis=1)
    o_ref[...] = res + _dot(o * _silu(g), w_ref[...])


def gla_out(o_f, o_b, head_nw, out_w, res, mix_nw, w_g):
    t, vd = o_f.shape
    d = out_w.shape[1]
    tm = min(ROW_TILE, t)
    row = lambda i: (i, 0)
    const = lambda i: (0, 0)
    return pl.pallas_call(
        _gla_out_kernel,
        grid=(t // tm,),
        in_specs=[pl.BlockSpec((tm, vd), row), pl.BlockSpec((tm, vd), row),
                  pl.BlockSpec((1, vd // GLA_HEADS), const),
                  pl.BlockSpec((vd, d), const), pl.BlockSpec((tm, d), row),
                  pl.BlockSpec((1, d), const), pl.BlockSpec((d, vd), const)],
        out_specs=pl.BlockSpec((tm, d), row),
        out_shape=jax.ShapeDtypeStruct((t, d), F32),
        compiler_params=_cparams(("parallel",)),
        name="gla_out",
    )(o_f, o_b, head_nw, out_w, res, mix_nw, w_g)


def _hy_filter_kernel(w1_ref, b1_ref, w2_ref, b2_ref, w3_ref, freq_ref, delta_ref, o_ref, *, seq):
    tr = o_ref.shape[0]
    r = pl.program_id(1) * tr + lax.broadcasted_iota(jnp.int32, (tr, 1), 0)
    pos = jnp.where(r < seq, r, 2 * seq - r).astype(F32)
    t = pos / (seq - 1.0)
    w = (2.0 * math.pi) * pos / seq
    band = lax.broadcasted_iota(jnp.int32, (1, HY_BANDS), 1).astype(F32)
    bands = 1e-4 + band * ((HY_BANDS - 1 - 1e-4) / (HY_BANDS - 1))
    zw1 = (t * w1_ref[0:1, :] + _dot_exact(jnp.cos(bands * w), w1_ref[1:1 + HY_BANDS, :])
           + _dot_exact(-jnp.sin(bands * w), w1_ref[1 + HY_BANDS:1 + 2 * HY_BANDS, :]))
    freq = freq_ref[...]
    hid = jnp.sin(freq * (zw1 + b1_ref[...]))
    hid = jnp.sin(freq * (_dot_exact(hid, w2_ref[...]) + b2_ref[...]))
    filt = _dot_exact(hid, w3_ref[...]) * jnp.exp(-t * delta_ref[...])
    o_ref[...] = jnp.where(r == seq, 0.0, filt)


def hy_filter(seq, w1, b1, w2, b2, w3, freq, width, tr=512):
    n = 2 * seq
    tr = min(tr, seq)
    tc = width
    emb, hidden = w1.shape
    min_decay = math.log(HY_DECAY_TARGET) / HY_SLOW_DECAY_PCT
    max_decay = math.log(HY_DECAY_TARGET) / HY_FAST_DECAY_PCT
    deltas = jnp.abs(jnp.linspace(min_decay, max_decay, width, dtype=F32))[None, :]
    nct = width // tc
    half = seq // tr
    const = lambda j, i: (0, 0)
    return pl.pallas_call(
        functools.partial(_hy_filter_kernel, seq=seq),
        grid=(nct, n // tr),
        in_specs=[pl.BlockSpec((emb, hidden), const), pl.BlockSpec((1, hidden), const),
                  pl.BlockSpec((hidden, hidden), const), pl.BlockSpec((1, hidden), const),
                  pl.BlockSpec((hidden, tc), lambda j, i: (0, j + nct * (i // half))),
                  pl.BlockSpec((1, hidden), const),
                  pl.BlockSpec((1, tc), lambda j, i: (0, j))],
        out_specs=pl.BlockSpec((tr, tc), lambda j, i: (i, j)),
        out_shape=jax.ShapeDtypeStruct((n, width), F32),
        compiler_params=_cparams(("parallel", "parallel")),
        name="hy_filter",
    )(w1, b1, w2, b2, w3, freq, deltas)


def _stack_complex(m):
    return np.concatenate([np.concatenate([m.real, -m.imag], axis=-1),
                           np.concatenate([m.imag, m.real], axis=-1)], axis=-2)


@functools.lru_cache(maxsize=None)
def _dft_tables(n, n1):
    n2 = n // n1
    h1 = n1 // 2
    ang = -2.0 * np.pi / n
    k1 = np.arange(n1)[None, :, None]
    nn1 = np.arange(n1)[None, None, :]
    nn2 = np.arange(n2)[:, None, None]
    g1_full = np.exp(1j * ang * k1 * (n2 * nn1 + nn2))
    g1 = _stack_complex(g1_full[:, :, :h1])
    g1_real = np.concatenate([g1_full.real, g1_full.imag], axis=-2)
    kk2 = np.arange(n2)[:, None]
    f2 = np.exp(-2j * np.pi * kk2 * np.arange(n2)[None, :] / n2)
    f2s = _stack_complex(f2)
    f3s = _stack_complex(np.conj(f2))
    out1 = np.arange(h1)[None, :, None]
    g4 = np.exp(-1j * ang * (n2 * out1 + nn2) * np.arange(n1)[None, None, :]) / n
    g4s = _stack_complex(g4)
    cast = lambda a: jnp.asarray(a, dtype=F32).astype(MXU_DTYPE)
    return cast(g1), cast(g1_real), cast(f2s), cast(f3s), cast(g4s)


FFT_UNROLL = 8
FFT_SLABS = 4
FFT_PAD = 8


def _load_complex_strided(ref, start, rows, stride):
    return jnp.concatenate([ref[0, pl.ds(start, rows, stride=stride), :],
                            ref[1, pl.ds(start, rows, stride=stride), :]], axis=0)


def _store_complex_slab(ref, base, rows, out):
    ref[0, pl.ds(base, rows), :] = out[:rows]
    ref[1, pl.ds(base, rows), :] = out[rows:]


def _shared_matrix_stage(load, store, mat_ref, count):
    def body(jj, carry):
        j0 = jj * FFT_SLABS
        x = jnp.concatenate([load(j0 + u) for u in range(FFT_SLABS)], axis=1).astype(MXU_DTYPE)
        out = jnp.dot(mat_ref[...], x, preferred_element_type=F32)
        for u in range(FFT_SLABS):
            store(j0 + u, out[:, u * LANES:(u + 1) * LANES])
        return carry

    lax.fori_loop(0, count // FFT_SLABS, body, 0, unroll=2)


def _per_slab_matrix_stage(load, store, mats_ref, count):
    def body(j, carry):
        store(j, jnp.dot(mats_ref[j], load(j).astype(MXU_DTYPE), preferred_element_type=F32))
        return carry

    lax.fori_loop(0, count, body, 0, unroll=FFT_UNROLL)


def _slab_base(j, rows):
    return pl.multiple_of(j * (rows + FFT_PAD), 8)


def _scratch_rows(n, n1):
    n2 = n // n1
    return max(n2 * (n1 + FFT_PAD), n1 * (n2 + FFT_PAD))


def _hy_spectrum_kernel(filt_ref, g1r_ref, f2s_ref, hf_ref, a_ref, *, n1):
    n = filt_ref.shape[0]
    n2 = n // n1
    _per_slab_matrix_stage(
        lambda j: filt_ref[pl.ds(j, n1, stride=n2), :],
        lambda j, out: _store_complex_slab(a_ref, _slab_base(j, n1), n1, out),
        g1r_ref, n2)
    _shared_matrix_stage(
        lambda j: _load_complex_strided(a_ref, j, n2, n1 + FFT_PAD),
        lambda j, out: _store_complex_slab(hf_ref, pl.multiple_of(j * n2, n2), n2, out),
        f2s_ref, n1)


def hy_spectrum(filt):
    n, c = filt.shape
    n1 = FFT_N1
    _, g1r, f2s, _, _ = _dft_tables(n, n1)
    return pl.pallas_call(
        functools.partial(_hy_spectrum_kernel, n1=n1),
        grid=(c // LANES,),
        in_specs=[pl.BlockSpec((n, LANES), lambda j: (0, j)),
                  pl.BlockSpec(g1r.shape, lambda j: (0, 0, 0)),
                  pl.BlockSpec(f2s.shape, lambda j: (0, 0))],
        out_specs=pl.BlockSpec((2, n, LANES), lambda j: (0, 0, j)),
        out_shape=jax.ShapeDtypeStruct((2, n, c), F32),
        scratch_shapes=[pltpu.VMEM((2, _scratch_rows(n, n1), LANES), F32)],
        compiler_params=_cparams(("parallel",)),
        name="hy_spectrum",
    )(filt, g1r, f2s)


def _hy_fftconv_kernel(v_ref, hf_ref, g1_ref, f2s_ref, f3s_ref, g4s_ref, o_ref, a_ref, y_ref, *, n1):
    seq = v_ref.shape[1]
    n = 2 * seq
    n2 = n // n1
    h1 = n1 // 2

    def multiply_by_spectrum(j, out):
        base = pl.multiple_of(j * n2, n2)
        xr, xi = out[:n2], out[n2:]
        hr = hf_ref[0, pl.ds(base, n2), :]
        hi = hf_ref[1, pl.ds(base, n2), :]
        y_ref[0, pl.ds(base, n2), :] = xr * hr - xi * hi
        y_ref[1, pl.ds(base, n2), :] = xr * hi + xi * hr

    def stage_input(m, carry):
        for comp in range(2):
            y_ref[comp, pl.ds(_slab_base(m, n2), n2), :] = v_ref[comp, pl.ds(pl.multiple_of(m * n2, n2), n2), :]
        return carry

    lax.fori_loop(0, h1, stage_input, 0)

    def stage_output(j, out):
        _store_complex_slab(y_ref, _slab_base(j, h1), h1, out)

    def unstage_output(m, carry):
        for comp in range(2):
            o_ref[comp, pl.ds(pl.multiple_of(m * n2, n2), n2), :] = y_ref[comp, pl.ds(m, n2, stride=h1 + FFT_PAD), :]
        return carry

    _per_slab_matrix_stage(
        lambda j: _load_complex_strided(y_ref, j, h1, n2 + FFT_PAD),
        lambda j, out: _store_complex_slab(a_ref, _slab_base(j, n1), n1, out),
        g1_ref, n2)
    _shared_matrix_stage(lambda j: _load_complex_strided(a_ref, j, n2, n1 + FFT_PAD), multiply_by_spectrum,
                         f2s_ref, n1)
    _shared_matrix_stage(
        lambda j: jnp.concatenate([y_ref[0, pl.ds(pl.multiple_of(j * n2, n2), n2), :],
                                   y_ref[1, pl.ds(pl.multiple_of(j * n2, n2), n2), :]], axis=0),
        lambda j, out: _store_complex_slab(a_ref, _slab_base(j, n2), n2, out),
        f3s_ref, n1)
    _per_slab_matrix_stage(lambda j: _load_complex_strided(a_ref, j, n1, n2 + FFT_PAD), stage_output, g4s_ref, n2)
    lax.fori_loop(0, h1, unstage_output, 0)


def hy_fftconv(vx, hf):
    bsz, seq, c = vx.shape
    n = 2 * seq
    n1 = FFT_N1
    g1, _, f2s, f3s, g4s = _dft_tables(n, n1)
    once = pl.Buffered(1)
    tab3 = lambda a: pl.BlockSpec(a.shape, lambda j, p: (0, 0, 0), pipeline_mode=once)
    tab2 = lambda a: pl.BlockSpec(a.shape, lambda j, p: (0, 0), pipeline_mode=once)
    pair = pl.BlockSpec((None, 2, seq, LANES), lambda j, p: (p, 0, 0, j))
    out = pl.pallas_call(
        functools.partial(_hy_fftconv_kernel, n1=n1),
        grid=(c // LANES, bsz // 2),
        in_specs=[pair,
                  pl.BlockSpec((2, n, LANES), lambda j, p: (0, 0, j), pipeline_mode=once),
                  tab3(g1), tab2(f2s), tab2(f3s), tab3(g4s)],
        out_specs=pair,
        out_shape=jax.ShapeDtypeStruct((bsz // 2, 2, seq, c), F32),
        scratch_shapes=[pltpu.VMEM((2, _scratch_rows(n, n1), LANES), F32), pltpu.VMEM((2, n, LANES), F32)],
        compiler_params=_cparams(("parallel", "parallel")),
        name="hy_fftconv",
    )(vx.reshape(bsz // 2, 2, seq, c), hf, g1, f2s, f3s, g4s)
    return out.reshape(bsz, seq, c)


def _hy_out_kernel(conv_ref, vx_ref, x0_ref, skip_ref, w_ref, b_ref, res_ref, o_ref):
    y = (conv_ref[...] + vx_ref[...] * skip_ref[...]) * x0_ref[...]
    o_ref[...] = res_ref[...] + _dot(y, w_ref[...]) + b_ref[...]


def hy_out(conv, vx, x0c, skip, out_w, out_b, res):
    t, c = vx.shape
    d = out_w.shape[1]
    tm = min(ROW_TILE, t)
    row = lambda i: (i, 0)
    const = lambda i: (0, 0)
    return pl.pallas_call(
        _hy_out_kernel,
        grid=(t // tm,),
        in_specs=[pl.BlockSpec((tm, c), row), pl.BlockSpec((tm, c), row), pl.BlockSpec((tm, c), row),
                  pl.BlockSpec((1, c), const), pl.BlockSpec((c, d), const), pl.BlockSpec((1, d), const),
                  pl.BlockSpec((tm, d), row)],
        out_specs=pl.BlockSpec((tm, d), row),
        out_shape=jax.ShapeDtypeStruct((t, d), F32),
        compiler_params=_cparams(("parallel",)),
        name="hy_out",
    )(conv, vx, x0c, skip, out_w, out_b, res)


def _pad_cols(a, width):
    return jnp.pad(a, ((0, 0), (0, width - a.shape[1])))


def _col_tile(n):
    for tile in (1024, 768, 512, 256, 128):
        if n % tile == 0:
            return tile
    raise ValueError(f"unsupported projection width {n}")


def mamba2_mixer(h, bsz, norm_w, in_w, conv_w, conv_b, dt_bias, a_log, d_skip, gn_w, out_w):
    t, d = h.shape
    seq = t // bsz
    heads = a_log.shape[1]
    d_inner = heads * SSD_HEADDIM
    conv_dim = conv_w.shape[1]
    main = d_inner + conv_dim
    w_dt = _pad_cols(in_w[:, main:], LANES).astype(MXU_DTYPE)
    xbc, dt_raw = norm_linear_conv(h, seq, norm_w, in_w[:, d_inner:main].astype(MXU_DTYPE), None,
                                   conv_w, conv_b[None, :], _col_tile(conv_dim),
                                   _silu_epilogue, (_col_tile(conv_dim),), w_side=w_dt)
    y_f, y_b = ssd_scan(xbc.reshape(bsz, seq, conv_dim), dt_raw.reshape(bsz, seq, LANES),
                        _pad_cols(dt_bias.reshape(1, 2 * heads), LANES),
                        _pad_cols(a_log.reshape(1, 2 * heads), LANES))
    return ssd_out(y_f.reshape(t, d_inner), y_b.reshape(t, d_inner), xbc, jnp.repeat(d_skip, SSD_HEADDIM)[None, :],
                   gn_w[None, :], out_w.astype(MXU_DTYPE), h, norm_w, in_w[:, :d_inner].astype(MXU_DTYPE))


def hyena_mixer(h, bsz, norm_w, in_w, in_b, conv_w, conv_b, f_w1, f_b1, f_w2, f_b2, f_w3, sin_freq, skip,
                out_w, out_b):
    t, d = h.shape
    seq = t // bsz
    width = skip.shape[0]
    sub = HY_PART_TILE
    by_tile = lambda a: a.reshape(a.shape[0], 3, width // sub, sub).transpose(0, 2, 1, 3).reshape(a.shape[0], 3 * width)
    x0c, vx = norm_linear_conv(h, seq, norm_w, by_tile(in_w).astype(MXU_DTYPE), by_tile(in_b[None, :]),
                               by_tile(conv_w), by_tile(conv_b[None, :]), 3 * sub, _hyena_epilogue, (sub, sub))
    filt = hy_filter(seq, f_w1, f_b1[None, :], f_w2, f_b2[None, :], f_w3, sin_freq[None, :], width)
    conv = hy_fftconv(vx.reshape(bsz, seq, width), hy_spectrum(filt))
    return hy_out(conv.reshape(t, width), vx, x0c, skip[None, :], out_w.astype(MXU_DTYPE), out_b[None, :], h)


def gla_mixer(h, bsz, norm_w, in_w, gk_w, gk_b, hn_w, out_w):
    t, d = h.shape
    seq = t // bsz
    rank, key_dim = gk_w.shape[1], gk_w.shape[2]
    value_dim = out_w.shape[0]
    qkv_dim = 2 * key_dim + value_dim
    main = qkv_dim + value_dim
    w_gl = _pad_cols(in_w[:, main:], LANES).astype(MXU_DTYPE)
    qkv, gl = norm_linear(h, norm_w, in_w[:, :qkv_dim].astype(MXU_DTYPE), w_gl, _col_tile(qkv_dim))
    wgk = jnp.zeros((LANES, 2 * key_dim), F32)
    wgk = wgk.at[:rank, :key_dim].set(gk_w[0]).at[rank:2 * rank, key_dim:].set(gk_w[1]).astype(MXU_DTYPE)
    o_f, o_b = gla_scan(qkv.reshape(bsz, seq, qkv_dim), gl.reshape(bsz, seq, LANES), wgk,
                        gk_b.reshape(1, 2 * key_dim), key_dim, value_dim)
    return gla_out(o_f.reshape(t, value_dim), o_b.reshape(t, value_dim), hn_w[None, :], out_w.astype(MXU_DTYPE), h,
                   norm_w, in_w[:, qkv_dim:main].astype(MXU_DTYPE))


def kernel(x, p, norm_mix, norm_ffn, norm_ple, ple_gate, ple_proj, ffn_w1, ffn_w3, ffn_w2, final_norm,
           ssd_in_w, ssd_conv_w, ssd_conv_b, ssd_dt_bias, ssd_a_log, ssd_d, ssd_norm, ssd_out_w,
           hy_in_w, hy_in_b, hy_conv_w, hy_conv_b, hy_f_w1, hy_f_b1, hy_f_w2, hy_f_b2, hy_f_w3,
           hy_sin_freq, hy_skip, hy_out_w, hy_out_b,
           gla_in_w, gla_gk_w, gla_gk_b, gla_norm, gla_out_w):
    bsz, seq, d = x.shape
    depth = p.shape[0]
    t = bsz * seq
    h = x.reshape(t, d)
    for i in range(depth):
        kind, j = i % N_MIXERS, i // N_MIXERS
        nw = norm_mix[i][None, :]
        if kind == 0:
            h = mamba2_mixer(h, bsz, nw, ssd_in_w[j], ssd_conv_w[j], ssd_conv_b[j], ssd_dt_bias[j], ssd_a_log[j],
                             ssd_d[j], ssd_norm[j], ssd_out_w[j])
        elif kind == 1:
            h = hyena_mixer(h, bsz, nw, hy_in_w[j], hy_in_b[j], hy_conv_w[j], hy_conv_b[j], hy_f_w1[j], hy_f_b1[j],
                            hy_f_w2[j], hy_f_b2[j], hy_f_w3[j], hy_sin_freq[j], hy_skip[j], hy_out_w[j], hy_out_b[j])
        else:
            h = gla_mixer(h, bsz, nw, gla_in_w[j], gla_gk_w[j], gla_gk_b[j], gla_norm[j], gla_out_w[j])
        h = ffn_ple(h, norm_ffn[i][None, :], ffn_w1[i].astype(MXU_DTYPE), ffn_w3[i].astype(MXU_DTYPE),
                    ffn_w2[i].astype(MXU_DTYPE), p[i].reshape(t, -1), norm_ple[i][None, :],
                    ple_gate[i].astype(MXU_DTYPE), ple_proj[i].astype(MXU_DTYPE), final_norm[None, :],
                    final=(i == depth - 1))
    return h.reshape(bsz, seq, d)
```
